```python
import math, functools
import jax
import jax.numpy as jnp
from jax import lax
import numpy as np

D_MODEL = 2048
BATCH = 4
SEQ = 4096
DEPTH = 2

GRID_W = 64
CTX_LEN = 256
Q_BLOCK = 128
ROPE_BASE = 10000.0
NORM_EPS = 1e-6

R_HEADS = 12
R_HEAD = 64
R_WIDTH = R_HEADS * R_HEAD
R_DECAY_RANK = 64
R_ICLR_RANK = 64
R_VRES_RANK = 32
R_GATE_RANK = 128
R_LN_EPS = 64e-5

G_HEADS = 6
G_KV_HEADS = 2
G_GROUP = G_HEADS // G_KV_HEADS
G_HEAD = 128
G_WIDTH = G_HEADS * G_HEAD
G_KV_WIDTH = G_KV_HEADS * G_HEAD

DF_HEADS = 6
DF_QK_HEAD = 64
DF_V_HEAD = 2 * DF_QK_HEAD
DF_QK_WIDTH = DF_HEADS * 2 * DF_QK_HEAD
DF_WIDTH = DF_HEADS * DF_V_HEAD
DF_SUBLN_EPS = 1e-5

N_BRANCH = 3
D_FF = 5504
CONV_W = 3

R_COLS = 3 * R_WIDTH + R_DECAY_RANK + R_ICLR_RANK + R_GATE_RANK
G_COLS = G_WIDTH + 2 * G_KV_WIDTH
DF_COLS = 2 * DF_QK_WIDTH + DF_WIDTH
GATE_COLS = N_BRANCH * D_MODEL
IN_COLS = R_COLS + G_COLS + DF_COLS + GATE_COLS
IN_SPLITS = (R_COLS, R_COLS + G_COLS, R_COLS + G_COLS + DF_COLS)
R_SPLITS = (R_WIDTH, 2 * R_WIDTH, 3 * R_WIDTH, 3 * R_WIDTH + R_DECAY_RANK,
            3 * R_WIDTH + R_DECAY_RANK + R_ICLR_RANK)

kernel_name = 'hybrid_rwkv7_gqa_diffattn_dit_block'


def rms_norm(x, gain, eps=NORM_EPS):
    xf = x.astype(jnp.float32)
    y = xf * lax.rsqrt(jnp.mean(xf * xf, axis=-1, keepdims=True) + eps)
    return (y * gain.astype(jnp.float32)).astype(x.dtype)


def ada_params(cond, w, bias):
    m = jax.nn.silu(cond) @ w + bias
    return jnp.split(m[..., None, :], 6, axis=-1)


def seg_neighbours(x):
    zero = jnp.zeros_like(x[:, :1])
    prev = jnp.concatenate([zero, x[:, :-1]], axis=1)
    nxt = jnp.concatenate([x[:, 1:], zero], axis=1)
    return prev, nxt


def centred_dwconv(x, w, bias):
    prev, nxt = seg_neighbours(x)
    return prev * w[0] + x * w[1] + nxt * w[2] + bias


def axial_rope_tables(row_idx, col_idx, head_dim):
    quarter = head_dim // 4
    inv = ROPE_BASE ** (-jnp.arange(quarter, dtype=jnp.float32) / quarter)
    ang_r = row_idx.astype(jnp.float32)[:, None] * inv
    ang_c = col_idx.astype(jnp.float32)[:, None] * inv
    return (jnp.cos(ang_r), jnp.sin(ang_r), jnp.cos(ang_c), jnp.sin(ang_c))


def _rotate_half(x, cos, sin):
    x1, x2 = jnp.split(x, 2, axis=-1)
    return jnp.concatenate([x1 * cos - x2 * sin, x1 * sin + x2 * cos], axis=-1)


def apply_axial_rope(x, tables):
    shape = lambda t: t.reshape((t.shape[0],) + (1,) * (x.ndim - 3) + (t.shape[1],)).astype(x.dtype)
    cos_r, sin_r, cos_c, sin_c = (shape(t) for t in tables)
    half = x.shape[-1] // 2
    return jnp.concatenate([_rotate_half(x[..., :half], cos_r, sin_r),
                            _rotate_half(x[..., half:], cos_c, sin_c)], axis=-1)


def sweep_query_blocks(block_fn, q):
    bsz, heads, groups, s, d = q.shape
    nb = s // Q_BLOCK
    qb = q.reshape(bsz, heads, groups, nb, Q_BLOCK, d).transpose(3, 0, 1, 2, 4, 5)
    ob = lax.map(block_fn, qb)
    return ob.transpose(1, 2, 3, 0, 4, 5).reshape(bsz, heads, ob.shape[3], s, ob.shape[-1])


def gqa_attend(q, k, v):
    s = jnp.einsum('bhgqd,bhkd->bhgqk', q, k).astype(jnp.float32) * (q.shape[-1] ** -0.5)
    p = jax.nn.softmax(s, axis=-1).astype(v.dtype)
    return jnp.einsum('bhgqk,bhkd->bhgqd', p, v)


def diff_attend(q, k, v, lam):
    s = jnp.einsum('bhmqd,bhmkd->bhmqk', q, k).astype(jnp.float32) * (q.shape[-1] ** -0.5)
    p = jax.nn.softmax(s, axis=-1)
    w = (p[:, :, 0] - lam * p[:, :, 1]).astype(v.dtype)
    return jnp.einsum('bhqk,bhkv->bhqv', w, v)[:, :, None]


def wkv7_scan(state0, r, w, k, v, a, b):
    def step(state, inp):
        r_t, w_t, k_t, v_t, a_t, b_t = inp
        sa = jnp.einsum('bhij,bhj->bhi', state, a_t)
        state = (state * w_t[:, :, None, :] + sa[..., None] * b_t[:, :, None, :]
                 + v_t[..., None] * k_t[:, :, None, :])
        return state, jnp.einsum('bhij,bhj->bhi', state, r_t)
    xs = tuple(u.astype(jnp.float32).transpose(1, 0, 2, 3) for u in (r, w, k, v, a, b))
    state, y = lax.scan(step, state0, xs)
    return y.transpose(1, 0, 2, 3), state


def rwkv_branch(pr, states, vres, mu, w0, w2, a0, a2, k_k, k_a, r_k, g2, ln_w, ln_b, need_out=True):
    f32 = jnp.float32
    bsz, t, _ = pr.shape
    heads = lambda u: u.reshape(bsz, t, R_HEADS, R_HEAD)
    prev, nxt = seg_neighbours(pr)
    pr = pr + mu * (0.5 * (prev + nxt) - pr)
    r, k, v, w_lo, a_lo, g_lo = jnp.split(pr, list(R_SPLITS), axis=-1)
    if vres is not None:
        v_first, v0, v1, v2 = vres
        v = v + (v_first - v) * jax.nn.sigmoid(v0 + (v @ v1) @ v2)
    kk = heads(k * k_k).astype(f32)
    kk = kk / jnp.maximum(jnp.sqrt(jnp.sum(kk * kk, axis=-1, keepdims=True)), 1e-12)
    rh, vh = heads(r.astype(f32)), heads(v.astype(f32))
    tw = jnp.tanh(w_lo)
    ys, k_dirs, finals = [], [], []
    for d in range(2):
        w_log = -jax.nn.softplus(-(w0[d] + tw @ w2[d]).astype(f32)) - 0.5
        decay = jnp.exp(-jnp.exp(w_log))
        a = jax.nn.sigmoid((a0[d] + a_lo @ a2[d]).astype(f32))
        k_d = heads(k.astype(f32) * (1.0 + (a - 1.0) * k_a.astype(f32)))
        ins = (rh, heads(decay), k_d, vh, -kk, kk * heads(a))
        if d == 1:
            ins = tuple(jnp.flip(u, axis=1) for u in ins)
        y_d, s_d = wkv7_scan(states[d], *ins)
        ys.append(y_d if d == 0 else jnp.flip(y_d, axis=1))
        k_dirs.append(k_d)
        finals.append(s_d)
    final_states = (finals[0], finals[1])
    if not need_out:
        return None, v, final_states
    y = ys[0] + ys[1]
    mean = jnp.mean(y, axis=-1, keepdims=True)
    var = jnp.mean(jnp.square(y - mean), axis=-1, keepdims=True)
    yn = ((y - mean) * lax.rsqrt(var + R_LN_EPS)).reshape(bsz, t, R_WIDTH)
    yn = yn * ln_w.astype(f32) + ln_b.astype(f32)
    bonus = jnp.sum(rh * (k_dirs[0] + k_dirs[1]) * r_k.astype(f32), axis=-1, keepdims=True) * vh
    g = jax.nn.sigmoid(g_lo) @ g2
    y = (yn + bonus.reshape(bsz, t, R_WIDTH)) * g
    return y.astype(pr.dtype), v, final_states


def gqa_heads(pg, q_gain, k_gain):
    bsz, t, _ = pg.shape
    q, k, v = jnp.split(pg, [G_WIDTH, G_WIDTH + G_KV_WIDTH], axis=-1)
    q = rms_norm(q.reshape(bsz, t, G_HEADS, G_HEAD), q_gain)
    k = rms_norm(k.reshape(bsz, t, G_KV_HEADS, G_HEAD), k_gain)
    return q, k, v.reshape(bsz, t, G_KV_HEADS, G_HEAD)


def gqa_q_layout(q):
    bsz, t = q.shape[:2]
    return q.reshape(bsz, t, G_KV_HEADS, G_GROUP, G_HEAD).transpose(0, 2, 3, 1, 4)


def gqa_out_layout(o):
    bsz, t = o.shape[0], o.shape[3]
    return o.transpose(0, 3, 1, 2, 4).reshape(bsz, t, G_WIDTH)


def diff_heads(pd):
    bsz, t, _ = pd.shape
    q, k, v = jnp.split(pd, [DF_QK_WIDTH, 2 * DF_QK_WIDTH], axis=-1)
    q = q.reshape(bsz, t, DF_HEADS, 2, DF_QK_HEAD)
    k = k.reshape(bsz, t, DF_HEADS, 2, DF_QK_HEAD)
    return q, k, v.reshape(bsz, t, DF_HEADS, DF_V_HEAD)


def diff_out(o, gain, lam_init):
    bsz, t = o.shape[0], o.shape[3]
    o = rms_norm(o[:, :, 0], gain, DF_SUBLN_EPS) * (1.0 - lam_init)
    return o.transpose(0, 2, 1, 3).reshape(bsz, t, DF_WIDTH)


def kv_layout(u):
    return jnp.swapaxes(u, 1, 2)


def gated_merge(gate_logits, y_r, y_g, y_d, wb_r, wb_g, wb_d, w_o):
    g_r, g_g, g_d = jnp.split(gate_logits, 3, axis=-1)
    m = (jax.nn.sigmoid(g_r) * (y_r @ wb_r) + jax.nn.sigmoid(g_g) * (y_g @ wb_g)
         + jax.nn.sigmoid(g_d) * (y_d @ wb_d))
    return m @ w_o


def conv_ffn(h, w_up, conv_w, conv_b, w_down):
    u = centred_dwconv(h @ w_up, conv_w, conv_b)
    val, gate = jnp.split(u, 2, axis=-1)
    return (val * jax.nn.silu(gate)) @ w_down


def setup_inputs(seed: int = 0) -> dict:
    key = jax.random.key(seed)
    keys = jax.random.split(key, 48)
    counter = [0]

    def nk():
        counter[0] += 1
        return keys[counter[0] - 1]

    f32 = jnp.float32
    nrm = lambda shape, scale: jax.random.normal(nk(), shape, f32) * scale
    uni = lambda shape: jax.random.uniform(nk(), shape, f32)
    gain = lambda shape: 1.0 + nrm(shape, 0.02)
    L, D = DEPTH, D_MODEL
    return {
        'x': nrm((BATCH, SEQ, D), 1.0),
        'c': nrm((BATCH, D), 1.0),
        'ctx': nrm((BATCH, CTX_LEN, D), 1.0),
        'c_ctx': nrm((D,), 1.0),
        'w_mod': nrm((L, D, 6 * D), 0.5 * D ** -0.5),
        'b_mod': nrm((L, 6 * D), 0.02),
        'norm1': gain((L, D)),
        'w_in': nrm((L, D, IN_COLS), D ** -0.5),
        'rwkv_mu': uni((L, R_COLS)),
        'rwkv_w0': -6.5 + 5.0 * uni((L, 2, R_WIDTH)),
        'rwkv_w2': nrm((L, 2, R_DECAY_RANK, R_WIDTH), 0.5 * R_DECAY_RANK ** -0.5),
        'rwkv_a0': nrm((L, 2, R_WIDTH), 0.1),
        'rwkv_a2': nrm((L, 2, R_ICLR_RANK, R_WIDTH), R_ICLR_RANK ** -0.5),
        'rwkv_k_k': 0.85 + nrm((L, R_WIDTH), 0.05),
        'rwkv_k_a': 1.0 + nrm((L, R_WIDTH), 0.05),
        'rwkv_r_k': nrm((L, R_HEADS, R_HEAD), 0.1),
        'rwkv_g2': nrm((L, R_GATE_RANK, R_WIDTH), R_GATE_RANK ** -0.5),
        'rwkv_ln_w': gain((L, R_WIDTH)),
        'rwkv_ln_b': nrm((L, R_WIDTH), 0.02),
        'rwkv_v0': nrm((L - 1, R_WIDTH), 0.1),
        'rwkv_v1': nrm((L - 1, R_WIDTH, R_VRES_RANK), R_WIDTH ** -0.5),
        'rwkv_v2': nrm((L - 1, R_VRES_RANK, R_WIDTH), R_VRES_RANK ** -0.5),
        'gqa_q_norm': gain((L, G_HEAD)),
        'gqa_k_norm': gain((L, G_HEAD)),
        'diff_lq1': nrm((L, DF_QK_HEAD), 0.1),
        'diff_lk1': nrm((L, DF_QK_HEAD), 0.1),
        'diff_lq2': nrm((L, DF_QK_HEAD), 0.1),
        'diff_lk2': nrm((L, DF_QK_HEAD), 0.1),
        'diff_subln': gain((L, DF_V_HEAD)),
        'w_branch_r': nrm((L, R_WIDTH, D), R_WIDTH ** -0.5),
        'w_branch_g': nrm((L, G_WIDTH, D), G_WIDTH ** -0.5),
        'w_branch_d': nrm((L, DF_WIDTH, D), DF_WIDTH ** -0.5),
        'w_out': nrm((L, D, D), D ** -0.5),
        'norm2': gain((L, D)),
        'ffn_up': nrm((L, D, 2 * D_FF), D ** -0.5),
        'ffn_conv_w': nrm((L, CONV_W, 2 * D_FF), CONV_W ** -0.5),
        'ffn_conv_b': nrm((L, 2 * D_FF), 0.02),
        'ffn_down': nrm((L, D_FF, D), D_FF ** -0.5),
        'final_norm': gain((D,)),
    }


def reference(x, c, ctx, c_ctx, w_mod, b_mod, norm1, w_in, rwkv_mu, rwkv_w0, rwkv_w2, rwkv_a0, rwkv_a2,
              rwkv_k_k, rwkv_k_a, rwkv_r_k, rwkv_g2, rwkv_ln_w, rwkv_ln_b, rwkv_v0, rwkv_v1, rwkv_v2,
              gqa_q_norm, gqa_k_norm, diff_lq1, diff_lk1, diff_lq2, diff_lk2, diff_subln,
              w_branch_r, w_branch_g, w_branch_d, w_out, norm2, ffn_up, ffn_conv_w, ffn_conv_b, ffn_down,
              final_norm):
    bsz, s, _ = x.shape
    rows = s // GRID_W
    row_idx = jnp.repeat(jnp.arange(rows, dtype=jnp.int32), GRID_W)
    col_idx = jnp.tile(jnp.arange(GRID_W, dtype=jnp.int32), rows)
    rope_g = axial_rope_tables(row_idx, col_idx, G_HEAD)
    rope_d = axial_rope_tables(row_idx, col_idx, DF_QK_HEAD)
    zero_state = jnp.zeros((ctx.shape[0], R_HEADS, R_HEAD, R_HEAD), jnp.float32)

    xc = ctx
    v_first, v_first_c = None, None
    for i in range(DEPTH):
        last = i == DEPTH - 1
        sh1, sc1, g1, sh2, sc2, g2 = ada_params(c, w_mod[i], b_mod[i])
        sh1c, sc1c, g1c, sh2c, sc2c, g2c = ada_params(c_ctx[None], w_mod[i], b_mod[i])

        h = rms_norm(x, norm1[i]) * (1.0 + sc1) + sh1
        hc = rms_norm(xc, norm1[i]) * (1.0 + sc1c) + sh1c
        p_r, p_g, p_d, p_gate = jnp.split(h @ w_in[i], list(IN_SPLITS), axis=-1)
        pc_r, pc_g, pc_d, pc_gate = jnp.split(hc @ w_in[i], list(IN_SPLITS), axis=-1)

        rw = (rwkv_mu[i], rwkv_w0[i], rwkv_w2[i], rwkv_a0[i], rwkv_a2[i], rwkv_k_k[i], rwkv_k_a[i],
              rwkv_r_k[i], rwkv_g2[i], rwkv_ln_w[i], rwkv_ln_b[i])
        vres = None if i == 0 else (v_first, rwkv_v0[i - 1], rwkv_v1[i - 1], rwkv_v2[i - 1])
        vres_c = None if i == 0 else (v_first_c, rwkv_v0[i - 1], rwkv_v1[i - 1], rwkv_v2[i - 1])
        y_rc, v_c, ctx_states = rwkv_branch(pc_r, (zero_state, zero_state), vres_c, *rw, need_out=not last)
        y_r, v_l, _ = rwkv_branch(p_r, ctx_states, vres, *rw)
        if i == 0:
            v_first, v_first_c = v_l, v_c

        q, k, v = gqa_heads(p_g, gqa_q_norm[i], gqa_k_norm[i])
        qc, kc, vc = gqa_heads(pc_g, gqa_q_norm[i], gqa_k_norm[i])
        q, k = apply_axial_rope(q, rope_g), apply_axial_rope(k, rope_g)
        k_all = jnp.concatenate([kv_layout(kc), kv_layout(k)], axis=2)
        v_all = jnp.concatenate([kv_layout(vc), kv_layout(v)], axis=2)
        y_g = gqa_out_layout(sweep_query_blocks(functools.partial(gqa_attend, k=k_all, v=v_all),
                                                gqa_q_layout(q)))

        lam_init = 0.8 - 0.6 * math.exp(-0.3 * i)
        lam = (jnp.exp(jnp.sum(diff_lq1[i].astype(jnp.float32) * diff_lk1[i].astype(jnp.float32)))
               - jnp.exp(jnp.sum(diff_lq2[i].astype(jnp.float32) * diff_lk2[i].astype(jnp.float32)))
               + lam_init)
        dq, dk, dv = diff_heads(p_d)
        dqc, dkc, dvc = diff_heads(pc_d)
        dq, dk = apply_axial_rope(dq, rope_d), apply_axial_rope(dk, rope_d)
        dk_all = jnp.concatenate([dkc.transpose(0, 2, 3, 1, 4), dk.transpose(0, 2, 3, 1, 4)], axis=3)
        dv_all = jnp.concatenate([kv_layout(dvc), kv_layout(dv)], axis=2)
        y_d = diff_out(sweep_query_blocks(functools.partial(diff_attend, k=dk_all, v=dv_all, lam=lam),
                                          dq.transpose(0, 2, 3, 1, 4)), diff_subln[i], lam_init)

        x = x + g1 * gated_merge(p_gate, y_r, y_g, y_d, w_branch_r[i], w_branch_g[i], w_branch_d[i], w_out[i])
        h2 = rms_norm(x, norm2[i]) * (1.0 + sc2) + sh2
        x = x + g2 * conv_ffn(h2, ffn_up[i], ffn_conv_w[i], ffn_conv_b[i], ffn_down[i])

        if not last:
            y_gc = gqa_out_layout(gqa_attend(gqa_q_layout(qc), kv_layout(kc), kv_layout(vc)))
            y_dc = diff_out(diff_attend(dqc.transpose(0, 2, 3, 1, 4), dkc.transpose(0, 2, 3, 1, 4),
                                        kv_layout(dvc), lam), diff_subln[i], lam_init)
            xc = xc + g1c * gated_merge(pc_gate, y_rc, y_gc, y_dc, w_branch_r[i], w_branch_g[i],
                                        w_branch_d[i], w_out[i])
            h2c = rms_norm(xc, norm2[i]) * (1.0 + sc2c) + sh2c
            xc = xc + g2c * conv_ffn(h2c, ffn_up[i], ffn_conv_w[i], ffn_conv_b[i], ffn_down[i])

    return rms_norm(x, final_norm)
```

```python
import functools
import math

import jax
import jax.numpy as jnp
from jax import lax
from jax.experimental import pallas as pl
from jax.experimental.pallas import tpu as pltpu

F32 = jnp.float32
BF16 = jnp.bfloat16
HI = lax.Precision.HIGHEST

GRID_W = 64
ROPE_BASE = 10000.0
NORM_EPS = 1e-6
R_HEADS, R_HEAD = 12, 64
R_WIDTH = R_HEADS * R_HEAD
R_DECAY_RANK, R_ICLR_RANK, R_GATE_RANK = 64, 64, 128
R_LN_EPS = 64e-5
G_HEADS, G_KV_HEADS, G_HEAD = 6, 2, 128
G_GROUP = G_HEADS // G_KV_HEADS
G_WIDTH = G_HEADS * G_HEAD
G_KV_WIDTH = G_KV_HEADS * G_HEAD
DF_HEADS, DF_QK_HEAD = 6, 64
DF_V_HEAD = 2 * DF_QK_HEAD
DF_WIDTH = DF_HEADS * DF_V_HEAD
DF_SUBLN_EPS = 1e-5
R_COLS = 3 * R_WIDTH + R_DECAY_RANK + R_ICLR_RANK + R_GATE_RANK
G_COLS = G_WIDTH + 2 * G_KV_WIDTH
DF_COLS = 3 * DF_WIDTH

LANES = 128
SUB = 256
CHUNK = 64
HALO = 16
FF_TILE = 512
VMEM_LIMIT = 56 * 1024 * 1024


def _cparams(sem):
    return pltpu.CompilerParams(dimension_semantics=sem, vmem_limit_bytes=VMEM_LIMIT)


def _mm(a, b, prec=None):
    return lax.dot_general(a, b, (((1,), (0,)), ((), ())), precision=prec, preferred_element_type=F32)


def _mm_nt(a, b, prec=None):
    return lax.dot_general(a, b, (((1,), (1,)), ((), ())), precision=prec, preferred_element_type=F32)


def _sigmoid(x):
    return 1.0 / (1.0 + jnp.exp(-x))


def _mod_kernel(c_ref, w_ref, b_ref, o_ref):
    c = c_ref[...]
    a = (c * _sigmoid(c)).astype(BF16)
    o_ref[...] = _mm(a, w_ref[...]) + b_ref[...]


def ada_mod(c_all, w_bf, b):
    rows, d = c_all.shape
    n = w_bf.shape[1]
    tn = 2048
    return pl.pallas_call(
        _mod_kernel,
        grid=(n // tn,),
        in_specs=[pl.BlockSpec((rows, d), lambda j: (0, 0)),
                  pl.BlockSpec((d, tn), lambda j: (0, j)),
                  pl.BlockSpec((1, tn), lambda j: (0, j))],
        out_specs=pl.BlockSpec((rows, tn), lambda j: (0, j)),
        out_shape=jax.ShapeDtypeStruct((rows, n), F32),
        compiler_params=_cparams(("parallel",)),
        name="ada_mod",
    )(c_all, w_bf, b.reshape(1, n))


def _modnorm(x, gain, sc, sh):
    ms = jnp.mean(x * x, axis=-1, keepdims=True)
    y = x * lax.rsqrt(ms + NORM_EPS) * gain
    return y * (1.0 + sc) + sh


def _normmm_kernel(x_ref, gain_ref, sc_ref, sh_ref, w_ref, o_ref, a_scr, *, nsub):
    @pl.when(pl.program_id(1) == 0)
    def _():
        for s in range(nsub):
            rows = pl.ds(s * SUB, SUB)
            a_scr[rows, :] = _modnorm(x_ref[rows, :], gain_ref[...], sc_ref[s], sh_ref[s]).astype(BF16)
    o_ref[...] = _mm(a_scr[...], w_ref[...])


def norm_matmul(x, gain, mod_t, sc_idx, sh_idx, w_bf, *, tm, tn):
    m, d = x.shape
    n = w_bf.shape[1]
    nsub = tm // SUB
    return pl.pallas_call(
        functools.partial(_normmm_kernel, nsub=nsub),
        grid=(m // tm, n // tn),
        in_specs=[pl.BlockSpec((tm, d), lambda i, j: (i, 0)),
                  pl.BlockSpec((1, d), lambda i, j: (0, 0)),
                  pl.BlockSpec((nsub, 1, d), lambda i, j: (i, 0, sc_idx)),
                  pl.BlockSpec((nsub, 1, d), lambda i, j: (i, 0, sh_idx)),
                  pl.BlockSpec((d, tn), lambda i, j: (0, j))],
        out_specs=pl.BlockSpec((tm, tn), lambda i, j: (i, j)),
        out_shape=jax.ShapeDtypeStruct((m, n), F32),
        scratch_shapes=[pltpu.VMEM((tm, d), BF16)],
        compiler_params=_cparams(("parallel", "arbitrary")),
        name="norm_matmul",
    )(x, gain.reshape(1, d), mod_t, mod_t, w_bf)


def _lane_partner(x, q):
    n = x.shape[-1]
    lane = lax.broadcasted_iota(jnp.int32, x.shape, x.ndim - 1)
    up = pltpu.roll(x, n - q, x.ndim - 1)
    dn = pltpu.roll(x, q, x.ndim - 1)
    return jnp.where((lane & q) == 0, up, dn)


def _group_allsum(x, group):
    s = 1
    while s < group:
        x = x + _lane_partner(x, s)
        s *= 2
    return x


def _attn_prep_kernel(pg_ref, pdq_ref, pdk_ref, pdv_ref, cg_ref, sg_ref, cd_ref, sd_ref,
                      qn_ref, kn_ref, qg_ref, kg_ref, vg_ref, dq_ref, dk_ref, dv_ref):
    cg, sg = cg_ref[...], sg_ref[...]

    def head_norm_rope(x, gain, scale):
        ms = jnp.mean(x * x, axis=-1, keepdims=True)
        y = x * lax.rsqrt(ms + NORM_EPS) * gain
        y = y * cg + _lane_partner(y, G_HEAD // 4) * sg
        return y * scale

    g_scale = G_HEAD ** -0.5
    for h in range(G_HEADS):
        cols = slice(h * G_HEAD, (h + 1) * G_HEAD)
        qg_ref[:, cols] = head_norm_rope(pg_ref[:, cols], qn_ref[...], g_scale).astype(BF16)
    for h in range(G_KV_HEADS):
        cols = slice(h * G_HEAD, (h + 1) * G_HEAD)
        src = slice(G_WIDTH + h * G_HEAD, G_WIDTH + (h + 1) * G_HEAD)
        kg_ref[:, cols] = head_norm_rope(pg_ref[:, src], kn_ref[...], 1.0).astype(BF16)
    vg_ref[...] = pg_ref[:, G_WIDTH + G_KV_WIDTH:].astype(BF16)

    cd, sd = cd_ref[...], sd_ref[...]
    d_scale = DF_QK_HEAD ** -0.5
    for h in range(DF_HEADS):
        cols = slice(h * DF_V_HEAD, (h + 1) * DF_V_HEAD)
        xq = pdq_ref[:, cols]
        dq_ref[:, cols] = ((xq * cd + _lane_partner(xq, DF_QK_HEAD // 4) * sd) * d_scale).astype(BF16)
        xk = pdk_ref[:, cols]
        dk_ref[:, cols] = (xk * cd + _lane_partner(xk, DF_QK_HEAD // 4) * sd).astype(BF16)
    dv_ref[...] = pdv_ref[...].astype(BF16)


def attn_prep(p, tabs, q_gain, k_gain, *, seg_tiles):
    m = p.shape[0]
    cg, sg, cd, sd = tabs
    g_blk = R_COLS // G_COLS
    d_blk = (R_COLS + G_COLS) // DF_WIDTH
    assert g_blk * G_COLS == R_COLS and d_blk * DF_WIDTH == R_COLS + G_COLS
    row = lambda i: (i, 0)
    tab = lambda i: (i % seg_tiles, 0)
    one = lambda i: (0, 0)
    outs = [jax.ShapeDtypeStruct((m, w), BF16) for w in (G_WIDTH, G_KV_WIDTH, G_KV_WIDTH, DF_WIDTH, DF_WIDTH, DF_WIDTH)]
    return pl.pallas_call(
        _attn_prep_kernel,
        grid=(m // SUB,),
        in_specs=[pl.BlockSpec((SUB, G_COLS), lambda i: (i, g_blk)),
                  pl.BlockSpec((SUB, DF_WIDTH), lambda i: (i, d_blk)),
                  pl.BlockSpec((SUB, DF_WIDTH), lambda i: (i, d_blk + 1)),
                  pl.BlockSpec((SUB, DF_WIDTH), lambda i: (i, d_blk + 2)),
                  pl.BlockSpec((SUB, LANES), tab), pl.BlockSpec((SUB, LANES), tab),
                  pl.BlockSpec((SUB, LANES), tab), pl.BlockSpec((SUB, LANES), tab),
                  pl.BlockSpec((1, G_HEAD), one), pl.BlockSpec((1, G_HEAD), one)],
        out_specs=[pl.BlockSpec((SUB, o.shape[1]), row) for o in outs],
        out_shape=outs,
        compiler_params=_cparams(("parallel",)),
        name="attn_prep",
    )(p, p, p, p, cg, sg, cd, sd, q_gain.reshape(1, G_HEAD), k_gain.reshape(1, G_HEAD))


def _gqa_kernel(q_ref, k_ref, v_ref, o_ref, *, ctx_len):
    q = q_ref[...]

    def attend(klen):
        s = _mm_nt(q, k_ref[0:klen, :])
        e = jnp.exp(s - jnp.max(s, axis=-1, keepdims=True))
        l = jnp.sum(e, axis=-1, keepdims=True)
        o_ref[...] = (_mm(e.astype(BF16), v_ref[0:klen, :]) / l).astype(o_ref.dtype)

    is_ctx = pl.program_id(2) == 0
    pl.when(is_ctx)(lambda: attend(ctx_len))
    pl.when(jnp.logical_not(is_ctx))(lambda: attend(k_ref.shape[0]))


def gqa_attention(qg, kg, vg, *, bsz, seg, ctx_len):
    m = qg.shape[0]
    seg_tiles = seg // SUB
    k3 = kg.reshape(bsz, seg, G_KV_WIDTH)
    v3 = vg.reshape(bsz, seg, G_KV_WIDTH)
    qo = lambda b, h, t, g: (b * seg_tiles + t, h * G_GROUP + g)
    kv = lambda b, h, t, g: (b, 0, h)
    return pl.pallas_call(
        functools.partial(_gqa_kernel, ctx_len=ctx_len),
        grid=(bsz, G_KV_HEADS, seg_tiles, G_GROUP),
        in_specs=[pl.BlockSpec((SUB, G_HEAD), qo),
                  pl.BlockSpec((None, seg, G_HEAD), kv),
                  pl.BlockSpec((None, seg, G_HEAD), kv)],
        out_specs=pl.BlockSpec((SUB, G_HEAD), qo),
        out_shape=jax.ShapeDtypeStruct((m, G_WIDTH), BF16),
        compiler_params=_cparams(("parallel", "parallel", "parallel", "parallel")),
        name="gqa_attention",
    )(qg, k3, v3)


def _diff_kernel(q_ref, k_ref, v_ref, lq1_ref, lk1_ref, lq2_ref, lk2_ref, gain_ref, o_ref, *, ctx_len, lam_init):
    q = q_ref[...]
    lane = lax.broadcasted_iota(jnp.int32, q.shape, 1)
    zero = jnp.zeros_like(q)
    q0 = jnp.where(lane < DF_QK_HEAD, q, zero)
    q1 = jnp.where(lane >= DF_QK_HEAD, q, zero)
    lam = (jnp.exp(jnp.sum(lq1_ref[...] * lk1_ref[...], axis=-1, keepdims=True))
           - jnp.exp(jnp.sum(lq2_ref[...] * lk2_ref[...], axis=-1, keepdims=True)) + lam_init)

    def attend(klen):
        k = k_ref[0:klen, :]
        s0 = _mm_nt(q0, k)
        s1 = _mm_nt(q1, k)
        e0 = jnp.exp(s0 - jnp.max(s0, axis=-1, keepdims=True))
        e1 = jnp.exp(s1 - jnp.max(s1, axis=-1, keepdims=True))
        c0 = 1.0 / jnp.sum(e0, axis=-1, keepdims=True)
        c1 = lam / jnp.sum(e1, axis=-1, keepdims=True)
        w = (e0 * c0 - e1 * c1).astype(BF16)
        o = _mm(w, v_ref[0:klen, :])
        ms = jnp.mean(o * o, axis=-1, keepdims=True)
        o = o * lax.rsqrt(ms + DF_SUBLN_EPS) * gain_ref[...] * (1.0 - lam_init)
        o_ref[...] = o.astype(o_ref.dtype)

    is_ctx = pl.program_id(2) == 0
    pl.when(is_ctx)(lambda: attend(ctx_len))
    pl.when(jnp.logical_not(is_ctx))(lambda: attend(k_ref.shape[0]))


def diff_attention(dq, dk, dv, lq1, lk1, lq2, lk2, gain, *, bsz, seg, ctx_len, lam_init):
    m = dq.shape[0]
    seg_tiles = seg // SUB
    k3 = dk.reshape(bsz, seg, DF_WIDTH)
    v3 = dv.reshape(bsz, seg, DF_WIDTH)
    qo = lambda b, h, t: (b * seg_tiles + t, h)
    kv = lambda b, h, t: (b, 0, h)
    one = lambda b, h, t: (0, 0)
    vec = pl.BlockSpec((1, DF_QK_HEAD), one)
    return pl.pallas_call(
        functools.partial(_diff_kernel, ctx_len=ctx_len, lam_init=lam_init),
        grid=(bsz, DF_HEADS, seg_tiles),
        in_specs=[pl.BlockSpec((SUB, DF_V_HEAD), qo),
                  pl.BlockSpec((None, seg, DF_V_HEAD), kv),
                  pl.BlockSpec((None, seg, DF_V_HEAD), kv),
                  vec, vec, vec, vec, pl.BlockSpec((1, DF_V_HEAD), one)],
        out_specs=pl.BlockSpec((SUB, DF_V_HEAD), qo),
        out_shape=jax.ShapeDtypeStruct((m, DF_WIDTH), BF16),
        compiler_params=_cparams(("parallel", "parallel", "parallel")),
        name="diff_attention",
    )(dq, k3, v3, lq1.reshape(1, -1), lk1.reshape(1, -1), lq2.reshape(1, -1), lk2.reshape(1, -1),
      gain.reshape(1, -1))


def _rwkv_prep_kernel(*refs, has_vres):
    (pr_ref, prev_ref, next_ref, pm_ref, nm_ref, mu_ref, w0_ref, w2_ref, a0_ref, a2_ref,
     kk_ref, ka_ref, rk_ref, g2_ref) = refs[:14]
    refs = refs[14:]
    if has_vres:
        vf_ref, v0_ref, v1_ref, v2_ref = refs[:4]
        refs = refs[4:]
    r_out, v_out, kn_out, lw_out, kd_out, b_out, bonus_out, g_out, vld_out = refs

    p = pr_ref[...]
    rows = lax.broadcasted_iota(jnp.int32, (SUB, 1), 0)
    prev = jnp.where(rows == 0, prev_ref[7:8, :], pltpu.roll(p, 1, 0)) * pm_ref[...]
    nxt = jnp.where(rows == SUB - 1, next_ref[0:1, :], pltpu.roll(p, SUB - 1, 0)) * nm_ref[...]
    xs = p + mu_ref[...] * (0.5 * (prev + nxt) - p)

    r = xs[:, 0:R_WIDTH]
    k = xs[:, R_WIDTH:2 * R_WIDTH]
    v = xs[:, 2 * R_WIDTH:3 * R_WIDTH]
    o = 3 * R_WIDTH
    w_lo = xs[:, o:o + R_DECAY_RANK]
    a_lo = xs[:, o + R_DECAY_RANK:o + R_DECAY_RANK + R_ICLR_RANK]
    g_lo = xs[:, o + R_DECAY_RANK + R_ICLR_RANK:]

    if has_vres:
        mix = _sigmoid(v0_ref[...] + _mm(_mm(v, v1_ref[...], HI), v2_ref[...], HI))
        v = v + (vf_ref[...] - v) * mix
    vld_out[...] = v

    kk = k * kk_ref[...]
    nrm = jnp.sqrt(_group_allsum(kk * kk, R_HEAD))
    kk = kk / jnp.maximum(nrm, 1e-12)
    tw = jnp.tanh(w_lo)
    ksum = jnp.zeros_like(k)
    dec_scale = math.exp(-0.5)
    for d in range(2):
        z = w0_ref[d:d + 1, :] + _mm(tw, w2_ref[d], HI)
        lw = -dec_scale * _sigmoid(z)
        a = _sigmoid(a0_ref[d:d + 1, :] + _mm(a_lo, a2_ref[d], HI))
        k_d = k * (1.0 + (a - 1.0) * ka_ref[...])
        b = kk * a
        ksum = ksum + k_d
        for h in range(R_HEADS):
            cols = slice(h * R_HEAD, (h + 1) * R_HEAD)
            lw_out[d, h] = lw[:, cols]
            kd_out[d, h] = k_d[:, cols]
            b_out[d, h] = b[:, cols]
    for h in range(R_HEADS):
        cols = slice(h * R_HEAD, (h + 1) * R_HEAD)
        r_out[h] = r[:, cols]
        v_out[h] = v[:, cols]
        kn_out[h] = kk[:, cols]
    bonus_out[...] = _group_allsum(r * ksum * rk_ref[...], R_HEAD) * v
    g_out[...] = _mm(_sigmoid(g_lo), g2_ref[...], HI)


def rwkv_prep(p, pmask, nmask, mu, w0, w2, a0, a2, k_k, k_a, r_k, g2, vres):
    m = p.shape[0]
    nt = m // SUB
    sub8 = SUB // 8
    last8 = m // 8 - 1
    one2 = lambda i: (0, 0)
    one3 = lambda i: (0, 0, 0)
    w = R_WIDTH
    in_specs = [pl.BlockSpec((SUB, R_COLS), lambda i: (i, 0)),
                pl.BlockSpec((8, R_COLS), lambda i: (jnp.maximum(i * sub8 - 1, 0), 0)),
                pl.BlockSpec((8, R_COLS), lambda i: (jnp.minimum((i + 1) * sub8, last8), 0)),
                pl.BlockSpec((SUB, 1), lambda i: (i, 0)),
                pl.BlockSpec((SUB, 1), lambda i: (i, 0)),
                pl.BlockSpec((1, R_COLS), one2),
                pl.BlockSpec((2, w), one2),
                pl.BlockSpec((2, R_DECAY_RANK, w), one3),
                pl.BlockSpec((2, w), one2),
                pl.BlockSpec((2, R_ICLR_RANK, w), one3),
                pl.BlockSpec((1, w), one2), pl.BlockSpec((1, w), one2), pl.BlockSpec((1, w), one2),
                pl.BlockSpec((R_GATE_RANK, w), one2)]
    args = [p, p, p, pmask, nmask, mu.reshape(1, -1), w0, w2, a0, a2, k_k.reshape(1, w), k_a.reshape(1, w),
            r_k.reshape(1, w), g2]
    if vres is not None:
        v_first, v0, v1, v2 = vres
        in_specs += [pl.BlockSpec((SUB, w), lambda i: (i, 0)), pl.BlockSpec((1, w), one2),
                     pl.BlockSpec(v1.shape, one2), pl.BlockSpec(v2.shape, one2)]
        args += [v_first, v0.reshape(1, w), v1, v2]
    hm = jax.ShapeDtypeStruct((R_HEADS, m, R_HEAD), F32)
    hm2 = jax.ShapeDtypeStruct((2, R_HEADS, m, R_HEAD), F32)
    ld = jax.ShapeDtypeStruct((m, w), F32)
    hm_spec = pl.BlockSpec((R_HEADS, SUB, R_HEAD), lambda i: (0, i, 0))
    hm2_spec = pl.BlockSpec((2, R_HEADS, SUB, R_HEAD), lambda i: (0, 0, i, 0))
    ld_spec = pl.BlockSpec((SUB, w), lambda i: (i, 0))
    return pl.pallas_call(
        functools.partial(_rwkv_prep_kernel, has_vres=vres is not None),
        grid=(nt,),
        in_specs=in_specs,
        out_specs=[hm_spec, hm_spec, hm_spec, hm2_spec, hm2_spec, hm2_spec, ld_spec, ld_spec, ld_spec],
        out_shape=[hm, hm, hm, hm2, hm2, hm2, ld, ld, ld],
        compiler_params=_cparams(("parallel",)),
        name="rwkv_prep",
    )(*args)


def _chunk_affine(r, v, kk, lw, k, b, strict, incl, eye):
    cum = _mm(incl.astype(F32), lw, HI)
    tot = jnp.sum(lw, axis=0, keepdims=True)
    e_out = jnp.exp(-cum)
    at = -kk * jnp.exp(cum - lw)
    rt = r * jnp.exp(cum)
    bt = b * e_out
    kt = k * e_out
    e_end = jnp.exp(tot - cum)
    bb = b * e_end
    kb = k * e_end
    lhs = jnp.concatenate([at, rt], axis=0)
    xb = _mm_nt(lhs, bt, HI)
    xk = _mm_nt(lhs, kt, HI)
    zero = jnp.zeros_like(eye)
    a_ab = jnp.where(strict, xb[:CHUNK], zero)
    a_rb = jnp.where(incl, xb[CHUNK:], zero)
    a_ak = jnp.where(strict, xk[:CHUNK], zero)
    a_rk = jnp.where(incl, xk[CHUNK:], zero)
    t = eye + a_ab
    pw = _mm(a_ab, a_ab, HI)
    n_sq = int(math.log2(CHUNK)) - 2
    for _ in range(n_sq):
        x = _mm(jnp.concatenate([t, pw], axis=0), pw, HI)
        t = t + x[:CHUNK]
        pw = x[CHUNK:]
    t = t + _mm(t, pw, HI)
    ah = _mm(t, at, HI)
    u0 = _mm(t, _mm(a_ak, v, HI), HI)
    rh = rt + _mm(a_rb, ah, HI)
    y0 = _mm(a_rb, u0, HI) + _mm(a_rk, v, HI)
    mm_ = eye * jnp.exp(tot) + _mm(ah.T, bb, HI)
    cm = _mm(u0.T, bb, HI) + _mm(v.T, kb, HI)
    return mm_, cm, rh, y0


def _rwkv_phase1_kernel(r_ref, v_ref, kn_ref, lw_ref, kd_ref, b_ref, m_out, c_out, rh_out, y0_out):
    d = pl.program_id(2)
    ti = lax.broadcasted_iota(jnp.int32, (CHUNK, CHUNK), 0)
    ii = lax.broadcasted_iota(jnp.int32, (CHUNK, CHUNK), 1)
    ahead = (ti - ii) * (1 - 2 * d)
    strict = ahead > 0
    incl = ahead >= 0
    eye = (ti == ii).astype(F32)

    def body(c, carry):
        rows = pl.ds(pl.multiple_of(c * CHUNK, CHUNK), CHUNK)
        mm_, cm, rh, y0 = _chunk_affine(r_ref[rows, :], v_ref[rows, :], kn_ref[rows, :], lw_ref[rows, :],
                                        kd_ref[rows, :], b_ref[rows, :], strict, incl, eye)
        m_out[rows, :] = mm_
        c_out[rows, :] = cm
        rh_out[rows, :] = rh
        y0_out[rows, :] = y0
        return carry

    lax.fori_loop(0, SUB // CHUNK, body, 0)


def rwkv_phase1(r, v, kn, lw, kd, b):
    heads, m, n = r.shape
    sh = lambda t, h, d: (h, t, 0)
    dr = lambda t, h, d: (d, h, t, 0)
    s_spec = pl.BlockSpec((None, SUB, n), sh)
    d_spec = pl.BlockSpec((None, None, SUB, n), dr)
    out = jax.ShapeDtypeStruct((2, heads, m, n), F32)
    return pl.pallas_call(
        _rwkv_phase1_kernel,
        grid=(m // SUB, heads, 2),
        in_specs=[s_spec, s_spec, s_spec, d_spec, d_spec, d_spec],
        out_specs=[d_spec] * 4,
        out_shape=[out] * 4,
        compiler_params=_cparams(("parallel", "parallel", "parallel")),
        name="rwkv_phase1",
    )(r, v, kn, lw, kd, b)


def _rwkv_phase2_kernel(mf_ref, cf_ref, mb_ref, cb_ref, sf_out, sb_out, s_scr):
    @pl.when(pl.program_id(0) == 0)
    def _():
        s_scr[...] = jnp.zeros_like(s_scr)

    heads, bsz = mf_ref.shape[0], mf_ref.shape[1]
    for d, (m_ref, c_ref, s_out) in enumerate(((mf_ref, cf_ref, sf_out), (mb_ref, cb_ref, sb_out))):
        def body(h, carry):
            for bi in range(bsz):
                s = s_scr[d, h, bi]
                s_out[h, bi] = s
                s_scr[d, h, bi] = _mm(s, m_ref[h, bi], HI) + c_ref[h, bi]
            return carry
        lax.fori_loop(0, heads, body, 0)


def rwkv_phase2(m_all, c_all, *, bsz, seg, ctx_len):
    _, heads, m, n = m_all.shape
    nchunk = seg // CHUNK
    nctx = ctx_len // CHUNK
    m5 = m_all.reshape(2, heads, bsz, seg, n)
    c5 = c_all.reshape(2, heads, bsz, seg, n)
    fwd = lambda s: s
    bwd = lambda s: jnp.where(s < nctx, nctx - 1 - s, nchunk - 1 + nctx - s)
    blk = (None, heads, bsz, CHUNK, n)
    in_f = pl.BlockSpec(blk, lambda s: (0, 0, 0, fwd(s), 0))
    in_b = pl.BlockSpec(blk, lambda s: (1, 0, 0, bwd(s), 0))
    oblk = (heads, bsz, CHUNK, n)
    out = jax.ShapeDtypeStruct((heads, bsz, seg, n), F32)
    sf, sb = pl.pallas_call(
        _rwkv_phase2_kernel,
        grid=(nchunk,),
        in_specs=[in_f, in_f, in_b, in_b],
        out_specs=[pl.BlockSpec(oblk, lambda s: (0, 0, fwd(s), 0)),
                   pl.BlockSpec(oblk, lambda s: (0, 0, bwd(s), 0))],
        out_shape=[out, out],
        scratch_shapes=[pltpu.VMEM((2, heads, bsz, n, n), F32)],
        compiler_params=_cparams(("arbitrary",)),
        name="rwkv_phase2",
    )(m5, c5, m5, c5)
    return sf.reshape(heads, m, n), sb.reshape(heads, m, n)


def _rwkv_phase3_kernel(rh_ref, y0_ref, sf_ref, sb_ref, bonus_ref, g_ref, lnw_ref, lnb_ref, o_ref, y_scr):
    heads = rh_ref.shape[1]

    def body(c, carry):
        rows = pl.ds(pl.multiple_of(c * CHUNK, CHUNK), CHUNK)
        for h in range(heads):
            y = (_mm_nt(rh_ref[0, h, rows, :], sf_ref[h, rows, :], HI) + y0_ref[0, h, rows, :]
                 + _mm_nt(rh_ref[1, h, rows, :], sb_ref[h, rows, :], HI) + y0_ref[1, h, rows, :])
            mean = jnp.mean(y, axis=-1, keepdims=True)
            yc = y - mean
            var = jnp.mean(yc * yc, axis=-1, keepdims=True)
            y_scr[rows, h * R_HEAD:(h + 1) * R_HEAD] = yc * lax.rsqrt(var + R_LN_EPS)
        return carry

    lax.fori_loop(0, SUB // CHUNK, body, 0)
    yn = y_scr[...] * lnw_ref[...] + lnb_ref[...]
    o_ref[...] = ((yn + bonus_ref[...]) * g_ref[...]).astype(o_ref.dtype)


def rwkv_phase3(rh, y0, sf, sb, bonus, g, ln_w, ln_b):
    _, heads, m, n = rh.shape
    w = heads * n
    d_spec = pl.BlockSpec((2, heads, SUB, n), lambda t: (0, 0, t, 0))
    s_spec = pl.BlockSpec((heads, SUB, n), lambda t: (0, t, 0))
    ld_spec = pl.BlockSpec((SUB, w), lambda t: (t, 0))
    one = pl.BlockSpec((1, w), lambda t: (0, 0))
    return pl.pallas_call(
        _rwkv_phase3_kernel,
        grid=(m // SUB,),
        in_specs=[d_spec, d_spec, s_spec, s_spec, ld_spec, ld_spec, one, one],
        out_specs=ld_spec,
        out_shape=jax.ShapeDtypeStruct((m, w), BF16),
        scratch_shapes=[pltpu.VMEM((SUB, w), F32)],
        compiler_params=_cparams(("parallel",)),
        name="rwkv_phase3",
    )(rh, y0, sf, sb, bonus, g, ln_w.reshape(1, w), ln_b.reshape(1, w))


def _merge_kernel(yr_ref, yg_ref, yd_ref, wr_ref, wg_ref, wd_ref, gr_ref, gg_ref, gd_ref, o_ref):
    acc = _sigmoid(gr_ref[...]) * _mm(yr_ref[...], wr_ref[...])
    acc = acc + _sigmoid(gg_ref[...]) * _mm(yg_ref[...], wg_ref[...])
    acc = acc + _sigmoid(gd_ref[...]) * _mm(yd_ref[...], wd_ref[...])
    o_ref[...] = acc.astype(o_ref.dtype)


def gated_merge(p, y_r, y_g, y_d, wr, wg, wd, *, tm, tn):
    m = p.shape[0]
    d = wr.shape[1]
    gate0 = (R_COLS + G_COLS + DF_COLS) // tn
    assert gate0 * tn == R_COLS + G_COLS + DF_COLS and d % tn == 0
    nb = d // tn
    y_spec = lambda w: pl.BlockSpec((tm, w), lambda i, j: (i, 0))
    w_spec = lambda w: pl.BlockSpec((w, tn), lambda i, j: (0, j))
    g_spec = lambda br: pl.BlockSpec((tm, tn), lambda i, j: (i, gate0 + br * nb + j))
    return pl.pallas_call(
        _merge_kernel,
        grid=(m // tm, nb),
        in_specs=[y_spec(R_WIDTH), y_spec(G_WIDTH), y_spec(DF_WIDTH),
                  w_spec(R_WIDTH), w_spec(G_WIDTH), w_spec(DF_WIDTH),
                  g_spec(0), g_spec(1), g_spec(2)],
        out_specs=pl.BlockSpec((tm, tn), lambda i, j: (i, j)),
        out_shape=jax.ShapeDtypeStruct((m, d), BF16),
        compiler_params=_cparams(("parallel", "parallel")),
        name="gated_merge",
    )(y_r, y_g, y_d, wr, wg, wd, p, p, p)


def _outproj_kernel(a_ref, w_ref, x_ref, g_ref, o_ref, *, nsub):
    y = _mm(a_ref[...], w_ref[...])
    for s in range(nsub):
        rows = pl.ds(s * SUB, SUB)
        o_ref[rows, :] = x_ref[rows, :] + g_ref[s] * y[s * SUB:(s + 1) * SUB, :]


def out_proj_residual(a, w_bf, x, mod_t, g_idx, *, tm, tn):
    m, d = x.shape
    nsub = tm // SUB
    gpb = d // tn
    return pl.pallas_call(
        functools.partial(_outproj_kernel, nsub=nsub),
        grid=(m // tm, d // tn),
        in_specs=[pl.BlockSpec((tm, a.shape[1]), lambda i, j: (i, 0)),
                  pl.BlockSpec((a.shape[1], tn), lambda i, j: (0, j)),
                  pl.BlockSpec((tm, tn), lambda i, j: (i, j)),
                  pl.BlockSpec((nsub, 1, tn), lambda i, j: (i, 0, g_idx * gpb + j))],
        out_specs=pl.BlockSpec((tm, tn), lambda i, j: (i, j)),
        out_shape=jax.ShapeDtypeStruct((m, d), F32),
        compiler_params=_cparams(("parallel", "parallel")),
        name="out_proj_residual",
    )(a, w_bf, x, mod_t)


def _ffn_kernel(x_ref, xp_ref, xn_ref, gain_ref, sc_ref, sh_ref, g_ref, pm_ref, nm_ref,
                wv_ref, wg_ref, cwv_ref, cwg_ref, cbv_ref, cbg_ref, wd_ref, fin_ref,
                o_ref, h_scr, acc_scr, uv_scr, ug_scr, *, nsub, final_norm):
    j = pl.program_id(1)
    tm = nsub * SUB

    @pl.when(j == 0)
    def _():
        gain = gain_ref[...]
        h_scr[0:HALO, :] = _modnorm(xp_ref[...], gain, sc_ref[0], sh_ref[0]).astype(BF16)
        for s in range(nsub):
            h_scr[pl.ds(HALO + s * SUB, SUB), :] = _modnorm(
                x_ref[pl.ds(s * SUB, SUB), :], gain, sc_ref[s], sh_ref[s]).astype(BF16)
        h_scr[HALO + tm:, :] = _modnorm(xn_ref[...], gain, sc_ref[nsub - 1], sh_ref[nsub - 1]).astype(BF16)
        acc_scr[...] = jnp.zeros_like(acc_scr)

    h = h_scr[...]
    uv_scr[...] = _mm(h, wv_ref[...])
    ug_scr[...] = _mm(h, wg_ref[...])
    pm, nm = pm_ref[...], nm_ref[...]

    def conv(u_scr, cw_ref, cb_ref):
        prev = u_scr[pl.ds(HALO - 1, tm), :] * pm
        cur = u_scr[pl.ds(HALO, tm), :]
        nxt = u_scr[pl.ds(HALO + 1, tm), :] * nm
        return prev * cw_ref[0:1, :] + cur * cw_ref[1:2, :] + nxt * cw_ref[2:3, :] + cb_ref[...]

    val = conv(uv_scr, cwv_ref, cbv_ref)
    gate = conv(ug_scr, cwg_ref, cbg_ref)
    act = (val * (gate * _sigmoid(gate))).astype(BF16)
    acc_scr[...] += _mm(act, wd_ref[...])

    @pl.when(j == pl.num_programs(1) - 1)
    def _():
        for s in range(nsub):
            rows = pl.ds(s * SUB, SUB)
            y = x_ref[rows, :] + g_ref[s] * acc_scr[rows, :]
            if final_norm:
                ms = jnp.mean(y * y, axis=-1, keepdims=True)
                y = y * lax.rsqrt(ms + NORM_EPS) * fin_ref[...]
            o_ref[rows, :] = y


def conv_ffn(x, gain, mod_t, sc_idx, sh_idx, g_idx, pmask, nmask, w_up, cw, cb, w_down, fin_gain,
             *, tm, final_norm):
    m, d = x.shape
    ffp = w_down.shape[0]
    nj = ffp // FF_TILE
    nsub = tm // SUB
    th = tm // HALO
    lasth = m // HALO - 1
    row = lambda i, j: (i, 0)
    one = lambda i, j: (0, 0)
    mod = lambda idx: pl.BlockSpec((nsub, 1, d), lambda i, j: (i, 0, idx))
    in_specs = [pl.BlockSpec((tm, d), row),
                pl.BlockSpec((HALO, d), lambda i, j: (jnp.maximum(i * th - 1, 0), 0)),
                pl.BlockSpec((HALO, d), lambda i, j: (jnp.minimum((i + 1) * th, lasth), 0)),
                pl.BlockSpec((1, d), one),
                mod(sc_idx), mod(sh_idx), mod(g_idx),
                pl.BlockSpec((tm, 1), row), pl.BlockSpec((tm, 1), row),
                pl.BlockSpec((d, FF_TILE), lambda i, j: (0, j)),
                pl.BlockSpec((d, FF_TILE), lambda i, j: (0, nj + j)),
                pl.BlockSpec((3, FF_TILE), lambda i, j: (0, j)),
                pl.BlockSpec((3, FF_TILE), lambda i, j: (0, nj + j)),
                pl.BlockSpec((1, FF_TILE), lambda i, j: (0, j)),
                pl.BlockSpec((1, FF_TILE), lambda i, j: (0, nj + j)),
                pl.BlockSpec((FF_TILE, d), lambda i, j: (j, 0)),
                pl.BlockSpec((1, d), one)]
    return pl.pallas_call(
        functools.partial(_ffn_kernel, nsub=nsub, final_norm=final_norm),
        grid=(m // tm, nj),
        in_specs=in_specs,
        out_specs=pl.BlockSpec((tm, d), row),
        out_shape=jax.ShapeDtypeStruct((m, d), F32),
        scratch_shapes=[pltpu.VMEM((tm + 2 * HALO, d), BF16), pltpu.VMEM((tm, d), F32),
                        pltpu.VMEM((tm + 2 * HALO, FF_TILE), F32), pltpu.VMEM((tm + 2 * HALO, FF_TILE), F32)],
        compiler_params=_cparams(("parallel", "arbitrary")),
        name="conv_ffn",
    )(x, x, x, gain.reshape(1, d), mod_t, mod_t, mod_t, pmask, nmask, w_up, w_up, cw, cw, cb, cb, w_down,
      fin_gain.reshape(1, d))


def _rope_tables(ctx_len, seq, head_dim):
    quarter = head_dim // 4
    t = jnp.arange(seq, dtype=jnp.int32)
    inv = ROPE_BASE ** (-jnp.arange(quarter, dtype=F32) / quarter)
    ang_r = (t // GRID_W).astype(F32)[:, None] * inv
    ang_c = (t % GRID_W).astype(F32)[:, None] * inv
    cos = jnp.concatenate([jnp.cos(ang_r)] * 2 + [jnp.cos(ang_c)] * 2, axis=-1)
    sin = jnp.concatenate([-jnp.sin(ang_r), jnp.sin(ang_r), -jnp.sin(ang_c), jnp.sin(ang_c)], axis=-1)
    reps = LANES // head_dim
    cos, sin = jnp.tile(cos, (1, reps)), jnp.tile(sin, (1, reps))
    cos = jnp.concatenate([jnp.ones((ctx_len, LANES), F32), cos], axis=0)
    sin = jnp.concatenate([jnp.zeros((ctx_len, LANES), F32), sin], axis=0)
    return cos, sin


def _pad_ffn(w_up, cw, cb, w_down, ffp):
    ff = w_down.shape[0]
    pad = ffp - ff
    halves = lambda a: jnp.concatenate([jnp.pad(a[..., :ff], [(0, 0)] * (a.ndim - 1) + [(0, pad)]),
                                        jnp.pad(a[..., ff:], [(0, 0)] * (a.ndim - 1) + [(0, pad)])], axis=-1)
    return (halves(w_up).astype(BF16), halves(cw), halves(cb[None, :]),
            jnp.pad(w_down, [(0, pad), (0, 0)]).astype(BF16))


def kernel(x, c, ctx, c_ctx, w_mod, b_mod, norm1, w_in, rwkv_mu, rwkv_w0, rwkv_w2, rwkv_a0, rwkv_a2, rwkv_k_k, rwkv_k_a, rwkv_r_k, rwkv_g2, rwkv_ln_w, rwkv_ln_b, rwkv_v0, rwkv_v1, rwkv_v2, gqa_q_norm, gqa_k_norm, diff_lq1, diff_lk1, diff_lq2, diff_lk2, diff_subln, w_branch_r, w_branch_g, w_branch_d, w_out, norm2, ffn_up, ffn_conv_w, ffn_conv_b, ffn_down, final_norm):
    bsz, seq, d = x.shape
    ctx_len = ctx.shape[1]
    depth = w_in.shape[0]
    seg = ctx_len + seq
    m = bsz * seg
    seg_tiles = seg // SUB
    assert ctx_len == SUB and seq % SUB == 0 and bsz + 1 <= 8

    xs = jnp.concatenate([ctx, x], axis=1).reshape(m, d)

    pos = jnp.arange(m, dtype=jnp.int32) % seg
    pmask = ((pos != 0) & (pos != ctx_len)).astype(F32)[:, None]
    nmask = ((pos != ctx_len - 1) & (pos != seg - 1)).astype(F32)[:, None]

    tabs = _rope_tables(ctx_len, seq, G_HEAD) + _rope_tables(ctx_len, seq, DF_QK_HEAD)

    c_all = jnp.zeros((8, d), F32).at[:bsz].set(c).at[bsz].set(c_ctx)
    sub_idx = jnp.arange(m // SUB, dtype=jnp.int32)
    mod_row = jnp.where(sub_idx % seg_tiles == 0, bsz, sub_idx // seg_tiles)

    ffp = -(-ffn_down.shape[1] // FF_TILE) * FF_TILE
    tm_big = 1024 if m % 1024 == 0 else SUB
    tm_ffn = 512 if m % 512 == 0 else SUB

    v_first = None
    for i in range(depth):
        last = i == depth - 1
        mod = ada_mod(c_all, w_mod[i].astype(BF16), b_mod[i])
        mod_t = mod[mod_row][:, None, :]

        p = norm_matmul(xs, norm1[i], mod_t, 1, 0, w_in[i].astype(BF16), tm=tm_big, tn=512)

        vres = None if i == 0 else (v_first, rwkv_v0[i - 1], rwkv_v1[i - 1], rwkv_v2[i - 1])
        r_h, v_h, kn_h, lw_h, kd_h, b_h, bonus, gate_r, v_ld = rwkv_prep(
            p, pmask, nmask, rwkv_mu[i], rwkv_w0[i], rwkv_w2[i], rwkv_a0[i], rwkv_a2[i],
            rwkv_k_k[i], rwkv_k_a[i], rwkv_r_k[i], rwkv_g2[i], vres)
        if i == 0:
            v_first = v_ld
        m_all, c_all_, rh, y0 = rwkv_phase1(r_h, v_h, kn_h, lw_h, kd_h, b_h)
        sf, sb = rwkv_phase2(m_all, c_all_, bsz=bsz, seg=seg, ctx_len=ctx_len)
        y_r = rwkv_phase3(rh, y0, sf, sb, bonus, gate_r, rwkv_ln_w[i], rwkv_ln_b[i])

        qg, kg, vg, dq, dk, dv = attn_prep(p, tabs, gqa_q_norm[i], gqa_k_norm[i], seg_tiles=seg_tiles)
        y_g = gqa_attention(qg, kg, vg, bsz=bsz, seg=seg, ctx_len=ctx_len)
        lam_init = 0.8 - 0.6 * math.exp(-0.3 * i)
        y_d = diff_attention(dq, dk, dv, diff_lq1[i], diff_lk1[i], diff_lq2[i], diff_lk2[i], diff_subln[i],
                             bsz=bsz, seg=seg, ctx_len=ctx_len, lam_init=lam_init)

        mrg = gated_merge(p, y_r, y_g, y_d, w_branch_r[i].astype(BF16), w_branch_g[i].astype(BF16),
                          w_branch_d[i].astype(BF16), tm=tm_big, tn=512)
        xs = out_proj_residual(mrg, w_out[i].astype(BF16), xs, mod_t, 2, tm=tm_big, tn=512)

        w_up, cw, cb, w_down = _pad_ffn(ffn_up[i], ffn_conv_w[i], ffn_conv_b[i], ffn_down[i], ffp)
        xs = conv_ffn(xs, norm2[i], mod_t, 4, 3, 5, pmask, nmask, w_up, cw, cb, w_down, final_norm,
                      tm=tm_ffn, final_norm=last)

    return xs.reshape(bsz, seg, d)[:, ctx_len:, :]
```

```python
import functools
import math

import jax
import jax.numpy as jnp
from jax import lax
from jax.experimental import pallas as pl
from jax.experimental.pallas import tpu as pltpu

F32 = jnp.float32
BF16 = jnp.bfloat16
HI = lax.Precision.HIGHEST

GRID_W = 64
ROPE_BASE = 10000.0
NORM_EPS = 1e-6
R_HEADS, R_HEAD = 12, 64
R_WIDTH = R_HEADS * R_HEAD
R_DECAY_RANK, R_ICLR_RANK, R_GATE_RANK = 64, 64, 128
R_LN_EPS = 64e-5
G_HEADS, G_KV_HEADS, G_HEAD = 6, 2, 128
G_GROUP = G_HEADS // G_KV_HEADS
G_WIDTH = G_HEADS * G_HEAD
G_KV_WIDTH = G_KV_HEADS * G_HEAD
DF_HEADS, DF_QK_HEAD = 6, 64
DF_V_HEAD = 2 * DF_QK_HEAD
DF_WIDTH = DF_HEADS * DF_V_HEAD
DF_SUBLN_EPS = 1e-5
R_COLS = 3 * R_WIDTH + R_DECAY_RANK + R_ICLR_RANK + R_GATE_RANK
G_COLS = G_WIDTH + 2 * G_KV_WIDTH
DF_COLS = 3 * DF_WIDTH

LANES = 128
SUB = 256
CHUNK = 64
HALO = 16
FF_TILE = 512
VMEM_LIMIT = 56 * 1024 * 1024

PAIR = 2 * R_HEAD
N_PAIR = R_WIDTH // PAIR
assert PAIR == LANES and CHUNK == R_HEAD


def _cparams(sem):
    return pltpu.CompilerParams(dimension_semantics=sem, vmem_limit_bytes=VMEM_LIMIT)


def _mm(a, b, prec=None):
    return lax.dot_general(a, b, (((1,), (0,)), ((), ())), precision=prec, preferred_element_type=F32)


def _mm_nt(a, b, prec=None):
    return lax.dot_general(a, b, (((1,), (1,)), ((), ())), precision=prec, preferred_element_type=F32)


def _sigmoid(x):
    return 1.0 / (1.0 + jnp.exp(-x))


def _mod_kernel(c_ref, w_ref, b_ref, o_ref):
    c = c_ref[...]
    a = (c * _sigmoid(c)).astype(BF16)
    o_ref[...] = _mm(a, w_ref[...]) + b_ref[...]


def ada_mod(c_all, w_bf, b):
    rows, d = c_all.shape
    n = w_bf.shape[1]
    tn = 2048
    return pl.pallas_call(
        _mod_kernel,
        grid=(n // tn,),
        in_specs=[pl.BlockSpec((rows, d), lambda j: (0, 0)),
                  pl.BlockSpec((d, tn), lambda j: (0, j)),
                  pl.BlockSpec((1, tn), lambda j: (0, j))],
        out_specs=pl.BlockSpec((rows, tn), lambda j: (0, j)),
        out_shape=jax.ShapeDtypeStruct((rows, n), F32),
        compiler_params=_cparams(("parallel",)),
        name="ada_mod",
    )(c_all, w_bf, b.reshape(1, n))


def _modnorm(x, gain, sc, sh):
    ms = jnp.mean(x * x, axis=-1, keepdims=True)
    y = x * lax.rsqrt(ms + NORM_EPS) * gain
    return y * (1.0 + sc) + sh


def _normmm_kernel(x_ref, gain_ref, sc_ref, sh_ref, w_ref, o_ref, a_scr, *, nsub):
    @pl.when(pl.program_id(1) == 0)
    def _():
        for s in range(nsub):
            rows = pl.ds(s * SUB, SUB)
            a_scr[rows, :] = _modnorm(x_ref[rows, :], gain_ref[...], sc_ref[s], sh_ref[s]).astype(BF16)
    o_ref[...] = _mm(a_scr[...], w_ref[...])


def norm_matmul(x, gain, mod_t, sc_idx, sh_idx, w_bf, *, tm, tn):
    m, d = x.shape
    n = w_bf.shape[1]
    nsub = tm // SUB
    return pl.pallas_call(
        functools.partial(_normmm_kernel, nsub=nsub),
        grid=(m // tm, n // tn),
        in_specs=[pl.BlockSpec((tm, d), lambda i, j: (i, 0)),
                  pl.BlockSpec((1, d), lambda i, j: (0, 0)),
                  pl.BlockSpec((nsub, 1, d), lambda i, j: (i, 0, sc_idx)),
                  pl.BlockSpec((nsub, 1, d), lambda i, j: (i, 0, sh_idx)),
                  pl.BlockSpec((d, tn), lambda i, j: (0, j))],
        out_specs=pl.BlockSpec((tm, tn), lambda i, j: (i, j)),
        out_shape=jax.ShapeDtypeStruct((m, n), F32),
        scratch_shapes=[pltpu.VMEM((tm, d), BF16)],
        compiler_params=_cparams(("parallel", "arbitrary")),
        name="norm_matmul",
    )(x, gain.reshape(1, d), mod_t, mod_t, w_bf)


def _lane_partner(x, q):
    n = x.shape[-1]
    lane = lax.broadcasted_iota(jnp.int32, x.shape, x.ndim - 1)
    up = pltpu.roll(x, n - q, x.ndim - 1)
    dn = pltpu.roll(x, q, x.ndim - 1)
    return jnp.where((lane & q) == 0, up, dn)


def _group_allsum(x, group):
    s = 1
    while s < group:
        x = x + _lane_partner(x, s)
        s *= 2
    return x


def _attn_prep_kernel(pg_ref, pdq_ref, pdk_ref, pdv_ref, cg_ref, sg_ref, cd_ref, sd_ref,
                      qn_ref, kn_ref, qg_ref, kg_ref, vg_ref, dq_ref, dk_ref, dv_ref):
    cg, sg = cg_ref[...], sg_ref[...]

    def head_norm_rope(x, gain, scale):
        ms = jnp.mean(x * x, axis=-1, keepdims=True)
        y = x * lax.rsqrt(ms + NORM_EPS) * gain
        y = y * cg + _lane_partner(y, G_HEAD // 4) * sg
        return y * scale

    g_scale = G_HEAD ** -0.5
    for h in range(G_HEADS):
        cols = slice(h * G_HEAD, (h + 1) * G_HEAD)
        qg_ref[:, cols] = head_norm_rope(pg_ref[:, cols], qn_ref[...], g_scale).astype(BF16)
    for h in range(G_KV_HEADS):
        cols = slice(h * G_HEAD, (h + 1) * G_HEAD)
        src = slice(G_WIDTH + h * G_HEAD, G_WIDTH + (h + 1) * G_HEAD)
        kg_ref[:, cols] = head_norm_rope(pg_ref[:, src], kn_ref[...], 1.0).astype(BF16)
    vg_ref[...] = pg_ref[:, G_WIDTH + G_KV_WIDTH:].astype(BF16)

    cd, sd = cd_ref[...], sd_ref[...]
    d_scale = DF_QK_HEAD ** -0.5
    for h in range(DF_HEADS):
        cols = slice(h * DF_V_HEAD, (h + 1) * DF_V_HEAD)
        xq = pdq_ref[:, cols]
        dq_ref[:, cols] = ((xq * cd + _lane_partner(xq, DF_QK_HEAD // 4) * sd) * d_scale).astype(BF16)
        xk = pdk_ref[:, cols]
        dk_ref[:, cols] = (xk * cd + _lane_partner(xk, DF_QK_HEAD // 4) * sd).astype(BF16)
    dv_ref[...] = pdv_ref[...].astype(BF16)


def attn_prep(p, tabs, q_gain, k_gain, *, seg_tiles):
    m = p.shape[0]
    cg, sg, cd, sd = tabs
    g_blk = R_COLS // G_COLS
    d_blk = (R_COLS + G_COLS) // DF_WIDTH
    assert g_blk * G_COLS == R_COLS and d_blk * DF_WIDTH == R_COLS + G_COLS
    row = lambda i: (i, 0)
    tab = lambda i: (i % seg_tiles, 0)
    one = lambda i: (0, 0)
    outs = [jax.ShapeDtypeStruct((m, w), BF16) for w in (G_WIDTH, G_KV_WIDTH, G_KV_WIDTH, DF_WIDTH, DF_WIDTH, DF_WIDTH)]
    return pl.pallas_call(
        _attn_prep_kernel,
        grid=(m // SUB,),
        in_specs=[pl.BlockSpec((SUB, G_COLS), lambda i: (i, g_blk)),
                  pl.BlockSpec((SUB, DF_WIDTH), lambda i: (i, d_blk)),
                  pl.BlockSpec((SUB, DF_WIDTH), lambda i: (i, d_blk + 1)),
                  pl.BlockSpec((SUB, DF_WIDTH), lambda i: (i, d_blk + 2)),
                  pl.BlockSpec((SUB, LANES), tab), pl.BlockSpec((SUB, LANES), tab),
                  pl.BlockSpec((SUB, LANES), tab), pl.BlockSpec((SUB, LANES), tab),
                  pl.BlockSpec((1, G_HEAD), one), pl.BlockSpec((1, G_HEAD), one)],
        out_specs=[pl.BlockSpec((SUB, o.shape[1]), row) for o in outs],
        out_shape=outs,
        compiler_params=_cparams(("parallel",)),
        name="attn_prep",
    )(p, p, p, p, cg, sg, cd, sd, q_gain.reshape(1, G_HEAD), k_gain.reshape(1, G_HEAD))


def _gqa_kernel(q_ref, k_ref, v_ref, o_ref, *, ctx_len):
    q = q_ref[...]

    def attend(klen):
        s = _mm_nt(q, k_ref[0:klen, :])
        e = jnp.exp(s - jnp.max(s, axis=-1, keepdims=True))
        l = jnp.sum(e, axis=-1, keepdims=True)
        o_ref[...] = (_mm(e.astype(BF16), v_ref[0:klen, :]) / l).astype(o_ref.dtype)

    is_ctx = pl.program_id(2) == 0
    pl.when(is_ctx)(lambda: attend(ctx_len))
    pl.when(jnp.logical_not(is_ctx))(lambda: attend(k_ref.shape[0]))


def gqa_attention(qg, kg, vg, *, bsz, seg, ctx_len):
    m = qg.shape[0]
    seg_tiles = seg // SUB
    k3 = kg.reshape(bsz, seg, G_KV_WIDTH)
    v3 = vg.reshape(bsz, seg, G_KV_WIDTH)
    qo = lambda b, h, t, g: (b * seg_tiles + t, h * G_GROUP + g)
    kv = lambda b, h, t, g: (b, 0, h)
    return pl.pallas_call(
        functools.partial(_gqa_kernel, ctx_len=ctx_len),
        grid=(bsz, G_KV_HEADS, seg_tiles, G_GROUP),
        in_specs=[pl.BlockSpec((SUB, G_HEAD), qo),
                  pl.BlockSpec((None, seg, G_HEAD), kv),
                  pl.BlockSpec((None, seg, G_HEAD), kv)],
        out_specs=pl.BlockSpec((SUB, G_HEAD), qo),
        out_shape=jax.ShapeDtypeStruct((m, G_WIDTH), BF16),
        compiler_params=_cparams(("parallel", "parallel", "parallel", "parallel")),
        name="gqa_attention",
    )(qg, k3, v3)


def _diff_kernel(q_ref, k_ref, v_ref, lq1_ref, lk1_ref, lq2_ref, lk2_ref, gain_ref, o_ref, *, ctx_len, lam_init):
    q = q_ref[...]
    lane = lax.broadcasted_iota(jnp.int32, q.shape, 1)
    zero = jnp.zeros_like(q)
    q0 = jnp.where(lane < DF_QK_HEAD, q, zero)
    q1 = jnp.where(lane >= DF_QK_HEAD, q, zero)
    lam = (jnp.exp(jnp.sum(lq1_ref[...] * lk1_ref[...], axis=-1, keepdims=True))
           - jnp.exp(jnp.sum(lq2_ref[...] * lk2_ref[...], axis=-1, keepdims=True)) + lam_init)

    def attend(klen):
        k = k_ref[0:klen, :]
        s0 = _mm_nt(q0, k)
        s1 = _mm_nt(q1, k)
        e0 = jnp.exp(s0 - jnp.max(s0, axis=-1, keepdims=True))
        e1 = jnp.exp(s1 - jnp.max(s1, axis=-1, keepdims=True))
        c0 = 1.0 / jnp.sum(e0, axis=-1, keepdims=True)
        c1 = lam / jnp.sum(e1, axis=-1, keepdims=True)
        w = (e0 * c0 - e1 * c1).astype(BF16)
        o = _mm(w, v_ref[0:klen, :])
        ms = jnp.mean(o * o, axis=-1, keepdims=True)
        o = o * lax.rsqrt(ms + DF_SUBLN_EPS) * gain_ref[...] * (1.0 - lam_init)
        o_ref[...] = o.astype(o_ref.dtype)

    is_ctx = pl.program_id(2) == 0
    pl.when(is_ctx)(lambda: attend(ctx_len))
    pl.when(jnp.logical_not(is_ctx))(lambda: attend(k_ref.shape[0]))


def diff_attention(dq, dk, dv, lq1, lk1, lq2, lk2, gain, *, bsz, seg, ctx_len, lam_init):
    m = dq.shape[0]
    seg_tiles = seg // SUB
    k3 = dk.reshape(bsz, seg, DF_WIDTH)
    v3 = dv.reshape(bsz, seg, DF_WIDTH)
    qo = lambda b, h, t: (b * seg_tiles + t, h)
    kv = lambda b, h, t: (b, 0, h)
    one = lambda b, h, t: (0, 0)
    vec = pl.BlockSpec((1, DF_QK_HEAD), one)
    return pl.pallas_call(
        functools.partial(_diff_kernel, ctx_len=ctx_len, lam_init=lam_init),
        grid=(bsz, DF_HEADS, seg_tiles),
        in_specs=[pl.BlockSpec((SUB, DF_V_HEAD), qo),
                  pl.BlockSpec((None, seg, DF_V_HEAD), kv),
                  pl.BlockSpec((None, seg, DF_V_HEAD), kv),
                  vec, vec, vec, vec, pl.BlockSpec((1, DF_V_HEAD), one)],
        out_specs=pl.BlockSpec((SUB, DF_V_HEAD), qo),
        out_shape=jax.ShapeDtypeStruct((m, DF_WIDTH), BF16),
        compiler_params=_cparams(("parallel", "parallel", "parallel")),
        name="diff_attention",
    )(dq, k3, v3, lq1.reshape(1, -1), lk1.reshape(1, -1), lq2.reshape(1, -1), lk2.reshape(1, -1),
      gain.reshape(1, -1))


def _rwkv_prep_kernel(*refs, has_vres):
    (pr_ref, prev_ref, next_ref, pm_ref, nm_ref, mu_ref, w0_ref, w2_ref, a0_ref, a2_ref,
     kk_ref, ka_ref, rk_ref, g2_ref) = refs[:14]
    refs = refs[14:]
    if has_vres:
        vf_ref, v0_ref, v1_ref, v2_ref = refs[:4]
        refs = refs[4:]
    r_out, v_out, kn_out, lw_out, kd_out, b_out, bonus_out, g_out = refs

    p = pr_ref[...]
    rows = lax.broadcasted_iota(jnp.int32, (SUB, 1), 0)
    prev = jnp.where(rows == 0, prev_ref[7:8, :], pltpu.roll(p, 1, 0)) * pm_ref[...]
    nxt = jnp.where(rows == SUB - 1, next_ref[0:1, :], pltpu.roll(p, SUB - 1, 0)) * nm_ref[...]
    xs = p + mu_ref[...] * (0.5 * (prev + nxt) - p)

    r = xs[:, 0:R_WIDTH]
    k = xs[:, R_WIDTH:2 * R_WIDTH]
    v = xs[:, 2 * R_WIDTH:3 * R_WIDTH]
    o = 3 * R_WIDTH
    w_lo = xs[:, o:o + R_DECAY_RANK]
    a_lo = xs[:, o + R_DECAY_RANK:o + R_DECAY_RANK + R_ICLR_RANK]
    g_lo = xs[:, o + R_DECAY_RANK + R_ICLR_RANK:]

    if has_vres:
        mix = _sigmoid(v0_ref[...] + _mm(_mm(v, v1_ref[...], HI), v2_ref[...], HI))
        v = v + (vf_ref[...] - v) * mix

    kk = k * kk_ref[...]
    nrm = jnp.sqrt(_group_allsum(kk * kk, R_HEAD))
    kk = kk / jnp.maximum(nrm, 1e-12)
    tw = jnp.tanh(w_lo)
    ksum = jnp.zeros_like(k)
    dec_scale = math.exp(-0.5)
    for d in range(2):
        z = w0_ref[d:d + 1, :] + _mm(tw, w2_ref[d], HI)
        lw_out[d] = -dec_scale * _sigmoid(z)
        a = _sigmoid(a0_ref[d:d + 1, :] + _mm(a_lo, a2_ref[d], HI))
        k_d = k * (1.0 + (a - 1.0) * ka_ref[...])
        kd_out[d] = k_d
        b_out[d] = kk * a
        ksum = ksum + k_d
    r_out[...] = r
    v_out[...] = v
    kn_out[...] = kk
    bonus_out[...] = _group_allsum(r * ksum * rk_ref[...], R_HEAD) * v
    g_out[...] = _mm(_sigmoid(g_lo), g2_ref[...], HI)


def rwkv_prep(p, pmask, nmask, mu, w0, w2, a0, a2, k_k, k_a, r_k, g2, vres):
    m = p.shape[0]
    nt = m // SUB
    sub8 = SUB // 8
    last8 = m // 8 - 1
    one2 = lambda i: (0, 0)
    one3 = lambda i: (0, 0, 0)
    w = R_WIDTH
    in_specs = [pl.BlockSpec((SUB, R_COLS), lambda i: (i, 0)),
                pl.BlockSpec((8, R_COLS), lambda i: (jnp.maximum(i * sub8 - 1, 0), 0)),
                pl.BlockSpec((8, R_COLS), lambda i: (jnp.minimum((i + 1) * sub8, last8), 0)),
                pl.BlockSpec((SUB, 1), lambda i: (i, 0)),
                pl.BlockSpec((SUB, 1), lambda i: (i, 0)),
                pl.BlockSpec((1, R_COLS), one2),
                pl.BlockSpec((2, w), one2),
                pl.BlockSpec((2, R_DECAY_RANK, w), one3),
                pl.BlockSpec((2, w), one2),
                pl.BlockSpec((2, R_ICLR_RANK, w), one3),
                pl.BlockSpec((1, w), one2), pl.BlockSpec((1, w), one2), pl.BlockSpec((1, w), one2),
                pl.BlockSpec((R_GATE_RANK, w), one2)]
    args = [p, p, p, pmask, nmask, mu.reshape(1, -1), w0, w2, a0, a2, k_k.reshape(1, w), k_a.reshape(1, w),
            r_k.reshape(1, w), g2]
    if vres is not None:
        v_first, v0, v1, v2 = vres
        in_specs += [pl.BlockSpec((SUB, w), lambda i: (i, 0)), pl.BlockSpec((1, w), one2),
                     pl.BlockSpec(v1.shape, one2), pl.BlockSpec(v2.shape, one2)]
        args += [v_first, v0.reshape(1, w), v1, v2]
    ld = jax.ShapeDtypeStruct((m, w), F32)
    ld2 = jax.ShapeDtypeStruct((2, m, w), F32)
    ld_spec = pl.BlockSpec((SUB, w), lambda i: (i, 0))
    ld2_spec = pl.BlockSpec((2, SUB, w), lambda i: (0, i, 0))
    return pl.pallas_call(
        functools.partial(_rwkv_prep_kernel, has_vres=vres is not None),
        grid=(nt,),
        in_specs=in_specs,
        out_specs=[ld_spec, ld_spec, ld_spec, ld2_spec, ld2_spec, ld2_spec, ld_spec, ld_spec],
        out_shape=[ld, ld, ld, ld2, ld2, ld2, ld, ld],
        compiler_params=_cparams(("parallel",)),
        name="rwkv_prep",
    )(*args)


def _split2(a):
    hi = a.astype(BF16)
    lo = (a - hi.astype(F32)).astype(BF16)
    return hi, lo


def _split3(a):
    h1 = a.astype(BF16)
    r1 = a - h1.astype(F32)
    h2 = r1.astype(BF16)
    return h1, h2, (r1 - h2.astype(F32)).astype(BF16)


def _mm3(a, b, nt=False):
    f = _mm_nt if nt else _mm
    return f(a[0], b[0]) + (f(a[0], b[1]) + f(a[1], b[0]))


def _pair_consts(sign):
    t = lax.broadcasted_iota(jnp.int32, (CHUNK, PAIR), 0)
    lane = lax.broadcasted_iota(jnp.int32, (CHUNK, PAIR), 1)
    ahead = (t - (lane & (R_HEAD - 1))) * sign
    one, zero = jnp.ones((CHUNK, PAIR), F32), jnp.zeros((CHUNK, PAIR), F32)
    head0 = lane < R_HEAD
    t3 = lax.broadcasted_iota(jnp.int32, (CHUNK, 3 * CHUNK), 0)
    i3 = lax.broadcasted_iota(jnp.int32, (CHUNK, 3 * CHUNK), 1) & (CHUNK - 1)
    r2 = lax.broadcasted_iota(jnp.int32, (PAIR, PAIR), 0)
    l2 = lax.broadcasted_iota(jnp.int32, (PAIR, PAIR), 1)
    return dict(
        strict=ahead > 0, incl=ahead >= 0, eye=jnp.where(ahead == 0, one, zero), head0=head0,
        m0=jnp.where(head0, one, zero).astype(BF16), m1=jnp.where(head0, zero, one).astype(BF16),
        tri3=jnp.where((t3 - i3) * sign >= 0, 1.0, 0.0).astype(BF16),
        same_head=(r2 >= R_HEAD) == (l2 >= R_HEAD), eye2=(r2 == l2).astype(F32))


def _bd(x, c):
    return tuple(jnp.concatenate([u * c["m0"], u * c["m1"]], axis=0) for u in x)


def _chunks_affine(r, v, kk, lw, k, b, c):
    each = lambda f, *xs: [f(*a) for a in zip(*xs)]
    zero = jnp.zeros((CHUNK, PAIR), F32)
    bd = lambda x: _bd(x, c)
    mm3 = lambda x, y: _mm3(x, y)
    mm3_nt = lambda x, y: _mm3(x, y, nt=True)
    top = lambda x: x[:CHUNK]
    bot = lambda x: x[CHUNK:]
    stack = lambda x, y: jnp.concatenate([x, y], axis=0)

    cum = each(lambda x: _mm(c["tri3"], jnp.concatenate(_split3(x), axis=0)), lw)
    tot = each(lambda x: jnp.sum(x, axis=0, keepdims=True), lw)
    e_out = each(lambda x: jnp.exp(-x), cum)
    at = each(lambda kn, cu, x: -kn * jnp.exp(cu - x), kk, cum, lw)
    rt = each(lambda x, cu: x * jnp.exp(cu), r, cum)
    e_end = each(lambda t_, cu: jnp.exp(t_ - cu), tot, cum)
    v_bd = each(lambda x: bd(_split2(x)), v)
    lhs = each(lambda x, y: _split2(stack(x, y)), at, rt)
    xb = each(mm3_nt, lhs, each(lambda x, e: bd(_split2(x * e)), b, e_out))
    xk = each(mm3_nt, lhs, each(lambda x, e: bd(_split2(x * e)), k, e_out))
    a_ab = each(lambda x: jnp.where(c["strict"], top(x), zero), xb)
    a_rb = each(lambda x: _split2(jnp.where(c["incl"], bot(x), zero)), xb)
    a_ak = each(lambda x: _split2(jnp.where(c["strict"], top(x), zero)), xk)
    a_rk = each(lambda x: _split2(jnp.where(c["incl"], bot(x), zero)), xk)
    t = each(lambda x: c["eye"] + x, a_ab)
    a_s = each(_split2, a_ab)
    pw = each(mm3, a_s, each(bd, a_s))
    for _ in range(int(math.log2(CHUNK)) - 2):
        x = each(mm3, each(lambda x, y: _split2(stack(x, y)), t, pw), each(lambda y: bd(_split2(y)), pw))
        t = each(lambda t_, x_: t_ + top(x_), t, x)
        pw = each(bot, x)
    t = each(lambda t_, p_: t_ + mm3(_split2(t_), bd(_split2(p_))), t, pw)
    t_s = each(_split2, t)
    ah = each(mm3, t_s, each(lambda x: bd(_split2(x)), at))
    x1 = each(mm3, a_ak, v_bd)
    u0 = each(mm3, t_s, each(lambda x: bd(_split2(x)), x1))
    rh = each(lambda r_, a_, h_: r_ + mm3(a_, bd(_split2(h_))), rt, a_rb, ah)
    y0 = each(lambda a_, u_, k_, v_: mm3(a_, bd(_split2(u_))) + mm3(k_, v_), a_rb, u0, a_rk, v_bd)
    bb = each(lambda x, e: _split2(x * e), b, e_end)
    kb = each(lambda x, e: _split2(x * e), k, e_end)
    m_full = each(lambda h_, b_: mm3(_split2(h_.T), b_), ah, bb)
    m_bd = each(lambda x, t_: jnp.where(c["same_head"], x, jnp.zeros_like(x)) + c["eye2"] * jnp.exp(t_), m_full, tot)
    c_full = each(lambda u_, b_, v_, k_: mm3(_split2(u_.T), b_) + mm3(_split2(v_.T), k_), u0, bb, v, kb)
    cm = each(lambda x: jnp.where(c["head0"], top(x), bot(x)), c_full)
    return m_bd, cm, rh, y0


def _rwkv_phase1_kernel(r_ref, v_ref, kn_ref, lw_ref, kd_ref, b_ref,
                        mh_out, ml_out, c_out, rhh_out, rhl_out, y0_out):
    c = _pair_consts(1 - 2 * pl.program_id(2))
    n = SUB // CHUNK
    rows = [slice(ci * CHUNK, (ci + 1) * CHUNK) for ci in range(n)]
    load = lambda ref: [ref[rw, :] for rw in rows]
    m_bd, cm, rh, y0 = _chunks_affine(load(r_ref), load(v_ref), load(kn_ref), load(lw_ref),
                                      load(kd_ref), load(b_ref), c)
    for ci in range(n):
        mrows = slice(ci * PAIR, (ci + 1) * PAIR)
        mh_out[mrows, :], ml_out[mrows, :] = _split2(m_bd[ci])
        c_out[rows[ci], :] = cm[ci]
        rhh_out[rows[ci], :], rhl_out[rows[ci], :] = _split2(rh[ci])
        y0_out[rows[ci], :] = y0[ci]


def rwkv_phase1(r, v, kn, lw, kd, b):
    m, w = r.shape
    cpt = SUB // CHUNK
    sh = lambda t, pr, d: (t, pr)
    dr = lambda t, pr, d: (d, t, pr)
    s_spec = pl.BlockSpec((SUB, PAIR), sh)
    d_spec = pl.BlockSpec((None, SUB, PAIR), dr)
    m_spec = pl.BlockSpec((None, None, cpt * PAIR, PAIR), lambda t, pr, d: (d, pr, t, 0))
    m_shape = jax.ShapeDtypeStruct((2, N_PAIR, (m // CHUNK) * PAIR, PAIR), BF16)
    tok = lambda dt: jax.ShapeDtypeStruct((2, m, w), dt)
    return pl.pallas_call(
        _rwkv_phase1_kernel,
        grid=(m // SUB, N_PAIR, 2),
        in_specs=[s_spec, s_spec, s_spec, d_spec, d_spec, d_spec],
        out_specs=[m_spec, m_spec, d_spec, d_spec, d_spec, d_spec],
        out_shape=[m_shape, m_shape, tok(F32), tok(BF16), tok(BF16), tok(F32)],
        compiler_params=_cparams(("parallel", "parallel", "parallel")),
        name="rwkv_phase1",
    )(r, v, kn, lw, kd, b)


def _rwkv_phase2_kernel(mhf_ref, mlf_ref, cf_ref, mhb_ref, mlb_ref, cb_ref,
                        sfh_out, sfl_out, sbh_out, sbl_out, s_scr):
    @pl.when(pl.program_id(0) == 0)
    def _():
        s_scr[...] = jnp.zeros_like(s_scr)

    bsz = cf_ref.shape[0]
    dirs = ((mhf_ref, mlf_ref, cf_ref, sfh_out, sfl_out), (mhb_ref, mlb_ref, cb_ref, sbh_out, sbl_out))
    for d, (mh_ref, ml_ref, c_ref, sh_out, sl_out) in enumerate(dirs):
        for bi in range(bsz):
            for pr in range(N_PAIR):
                cols = slice(pr * PAIR, (pr + 1) * PAIR)
                s = _split2(s_scr[d, bi, :, cols])
                sh_out[bi, :, cols], sl_out[bi, :, cols] = s
                s_scr[d, bi, :, cols] = _mm3(s, (mh_ref[pr, bi], ml_ref[pr, bi])) + c_ref[bi, :, cols]


def rwkv_phase2(mh, ml, cm, *, bsz, seg, ctx_len):
    w = cm.shape[-1]
    nchunk = seg // CHUNK
    nctx = ctx_len // CHUNK
    mh5 = mh.reshape(2, N_PAIR, bsz, nchunk * PAIR, PAIR)
    ml5 = ml.reshape(2, N_PAIR, bsz, nchunk * PAIR, PAIR)
    c4 = cm.reshape(2, bsz, seg, w)
    fwd = lambda s: s
    bwd = lambda s: jnp.where(s < nctx, nctx - 1 - s, nchunk - 1 + nctx - s)
    mblk = (None, N_PAIR, bsz, PAIR, PAIR)
    cblk = (None, bsz, CHUNK, w)
    m_spec = lambda d, order: pl.BlockSpec(mblk, lambda s: (d, 0, 0, order(s), 0))
    c_spec = lambda d, order: pl.BlockSpec(cblk, lambda s: (d, 0, order(s), 0))
    o_spec = lambda order: pl.BlockSpec((bsz, CHUNK, w), lambda s: (0, order(s), 0))
    out = jax.ShapeDtypeStruct((bsz, seg, w), BF16)
    outs = pl.pallas_call(
        _rwkv_phase2_kernel,
        grid=(nchunk,),
        in_specs=[m_spec(0, fwd), m_spec(0, fwd), c_spec(0, fwd), m_spec(1, bwd), m_spec(1, bwd), c_spec(1, bwd)],
        out_specs=[o_spec(fwd), o_spec(fwd), o_spec(bwd), o_spec(bwd)],
        out_shape=[out] * 4,
        scratch_shapes=[pltpu.VMEM((2, bsz, R_HEAD, w), F32)],
        compiler_params=_cparams(("arbitrary",)),
        name="rwkv_phase2",
    )(mh5, ml5, c4, mh5, ml5, c4)
    return [o.reshape(bsz * seg, w) for o in outs]


def _rwkv_phase3_kernel(rhh_ref, rhl_ref, y0_ref, sfh_ref, sfl_ref, sbh_ref, sbl_ref,
                        bonus_ref, g_ref, lnw_ref, lnb_ref, o_ref, y_scr):
    lane = lax.broadcasted_iota(jnp.int32, (CHUNK, PAIR), 1)
    one, zero = jnp.ones((CHUNK, PAIR), F32), jnp.zeros((CHUNK, PAIR), F32)
    c = dict(m0=jnp.where(lane < R_HEAD, one, zero).astype(BF16),
             m1=jnp.where(lane < R_HEAD, zero, one).astype(BF16))
    states = ((sfh_ref, sfl_ref), (sbh_ref, sbl_ref))

    def body(ci, carry):
        rows = pl.ds(pl.multiple_of(ci * CHUNK, CHUNK), CHUNK)
        for pr in range(N_PAIR):
            cols = slice(pr * PAIR, (pr + 1) * PAIR)
            y = y0_ref[0, rows, cols] + y0_ref[1, rows, cols]
            for d, (sh_ref, sl_ref) in enumerate(states):
                y = y + _mm3((rhh_ref[d, rows, cols], rhl_ref[d, rows, cols]),
                             _bd((sh_ref[rows, cols], sl_ref[rows, cols]), c), nt=True)
            y_scr[rows, cols] = y
        return carry

    lax.fori_loop(0, SUB // CHUNK, body, 0)
    y = y_scr[...]
    inv_n = 1.0 / R_HEAD
    yc = y - _group_allsum(y, R_HEAD) * inv_n
    var = _group_allsum(yc * yc, R_HEAD) * inv_n
    yn = yc * lax.rsqrt(var + R_LN_EPS) * lnw_ref[...] + lnb_ref[...]
    o_ref[...] = ((yn + bonus_ref[...]) * g_ref[...]).astype(o_ref.dtype)


def rwkv_phase3(rhh, rhl, y0, sfh, sfl, sbh, sbl, bonus, g, ln_w, ln_b):
    _, m, w = y0.shape
    d_spec = pl.BlockSpec((2, SUB, w), lambda t: (0, t, 0))
    ld_spec = pl.BlockSpec((SUB, w), lambda t: (t, 0))
    one = pl.BlockSpec((1, w), lambda t: (0, 0))
    return pl.pallas_call(
        _rwkv_phase3_kernel,
        grid=(m // SUB,),
        in_specs=[d_spec, d_spec, d_spec, ld_spec, ld_spec, ld_spec, ld_spec, ld_spec, ld_spec, one, one],
        out_specs=ld_spec,
        out_shape=jax.ShapeDtypeStruct((m, w), BF16),
        scratch_shapes=[pltpu.VMEM((SUB, w), F32)],
        compiler_params=_cparams(("parallel",)),
        name="rwkv_phase3",
    )(rhh, rhl, y0, sfh, sfl, sbh, sbl, bonus, g, ln_w.reshape(1, w), ln_b.reshape(1, w))


def _merge_kernel(yr_ref, yg_ref, yd_ref, wr_ref, wg_ref, wd_ref, gr_ref, gg_ref, gd_ref, o_ref):
    acc = _sigmoid(gr_ref[...]) * _mm(yr_ref[...], wr_ref[...])
    acc = acc + _sigmoid(gg_ref[...]) * _mm(yg_ref[...], wg_ref[...])
    acc = acc + _sigmoid(gd_ref[...]) * _mm(yd_ref[...], wd_ref[...])
    o_ref[...] = acc.astype(o_ref.dtype)


def gated_merge(p, y_r, y_g, y_d, wr, wg, wd, *, tm, tn):
    m = p.shape[0]
    d = wr.shape[1]
    gate0 = (R_COLS + G_COLS + DF_COLS) // tn
    assert gate0 * tn == R_COLS + G_COLS + DF_COLS and d % tn == 0
    nb = d // tn
    y_spec = lambda w: pl.BlockSpec((tm, w), lambda i, j: (i, 0))
    w_spec = lambda w: pl.BlockSpec((w, tn), lambda i, j: (0, j))
    g_spec = lambda br: pl.BlockSpec((tm, tn), lambda i, j: (i, gate0 + br * nb + j))
    return pl.pallas_call(
        _merge_kernel,
        grid=(m // tm, nb),
        in_specs=[y_spec(R_WIDTH), y_spec(G_WIDTH), y_spec(DF_WIDTH),
                  w_spec(R_WIDTH), w_spec(G_WIDTH), w_spec(DF_WIDTH),
                  g_spec(0), g_spec(1), g_spec(2)],
        out_specs=pl.BlockSpec((tm, tn), lambda i, j: (i, j)),
        out_shape=jax.ShapeDtypeStruct((m, d), BF16),
        compiler_params=_cparams(("parallel", "parallel")),
        name="gated_merge",
    )(y_r, y_g, y_d, wr, wg, wd, p, p, p)


def _outproj_kernel(a_ref, w_ref, x_ref, g_ref, o_ref, *, nsub):
    y = _mm(a_ref[...], w_ref[...])
    for s in range(nsub):
        rows = pl.ds(s * SUB, SUB)
        o_ref[rows, :] = x_ref[rows, :] + g_ref[s] * y[s * SUB:(s + 1) * SUB, :]


def out_proj_residual(a, w_bf, x, mod_t, g_idx, *, tm, tn):
    m, d = x.shape
    nsub = tm // SUB
    gpb = d // tn
    return pl.pallas_call(
        functools.partial(_outproj_kernel, nsub=nsub),
        grid=(m // tm, d // tn),
        in_specs=[pl.BlockSpec((tm, a.shape[1]), lambda i, j: (i, 0)),
                  pl.BlockSpec((a.shape[1], tn), lambda i, j: (0, j)),
                  pl.BlockSpec((tm, tn), lambda i, j: (i, j)),
                  pl.BlockSpec((nsub, 1, tn), lambda i, j: (i, 0, g_idx * gpb + j))],
        out_specs=pl.BlockSpec((tm, tn), lambda i, j: (i, j)),
        out_shape=jax.ShapeDtypeStruct((m, d), F32),
        compiler_params=_cparams(("parallel", "parallel")),
        name="out_proj_residual",
    )(a, w_bf, x, mod_t)


def _ffn_kernel(x_ref, xp_ref, xn_ref, gain_ref, sc_ref, sh_ref, g_ref, pm_ref, nm_ref,
                wv_ref, wg_ref, cwv_ref, cwg_ref, cbv_ref, cbg_ref, wd_ref, fin_ref,
                o_ref, h_scr, acc_scr, uv_scr, ug_scr, *, nsub, final_norm):
    j = pl.program_id(1)
    tm = nsub * SUB

    @pl.when(j == 0)
    def _():
        gain = gain_ref[...]
        h_scr[0:HALO, :] = _modnorm(xp_ref[...], gain, sc_ref[0], sh_ref[0]).astype(BF16)
        for s in range(nsub):
            h_scr[pl.ds(HALO + s * SUB, SUB), :] = _modnorm(
                x_ref[pl.ds(s * SUB, SUB), :], gain, sc_ref[s], sh_ref[s]).astype(BF16)
        h_scr[HALO + tm:, :] = _modnorm(xn_ref[...], gain, sc_ref[nsub - 1], sh_ref[nsub - 1]).astype(BF16)
        acc_scr[...] = jnp.zeros_like(acc_scr)

    h = h_scr[...]
    uv_scr[...] = _mm(h, wv_ref[...])
    ug_scr[...] = _mm(h, wg_ref[...])
    pm, nm = pm_ref[...], nm_ref[...]

    def conv(u_scr, cw_ref, cb_ref):
        prev = u_scr[pl.ds(HALO - 1, tm), :] * pm
        cur = u_scr[pl.ds(HALO, tm), :]
        nxt = u_scr[pl.ds(HALO + 1, tm), :] * nm
        return prev * cw_ref[0:1, :] + cur * cw_ref[1:2, :] + nxt * cw_ref[2:3, :] + cb_ref[...]

    val = conv(uv_scr, cwv_ref, cbv_ref)
    gate = conv(ug_scr, cwg_ref, cbg_ref)
    act = (val * (gate * _sigmoid(gate))).astype(BF16)
    acc_scr[...] += _mm(act, wd_ref[...])

    @pl.when(j == pl.num_programs(1) - 1)
    def _():
        for s in range(nsub):
            rows = pl.ds(s * SUB, SUB)
            y = x_ref[rows, :] + g_ref[s] * acc_scr[rows, :]
            if final_norm:
                ms = jnp.mean(y * y, axis=-1, keepdims=True)
                y = y * lax.rsqrt(ms + NORM_EPS) * fin_ref[...]
            o_ref[rows, :] = y


def conv_ffn(x, gain, mod_t, sc_idx, sh_idx, g_idx, pmask, nmask, w_up, cw, cb, w_down, fin_gain,
             *, tm, final_norm):
    m, d = x.shape
    ffp = w_down.shape[0]
    nj = ffp // FF_TILE
    nsub = tm // SUB
    th = tm // HALO
    lasth = m // HALO - 1
    row = lambda i, j: (i, 0)
    one = lambda i, j: (0, 0)
    mod = lambda idx: pl.BlockSpec((nsub, 1, d), lambda i, j: (i, 0, idx))
    in_specs = [pl.BlockSpec((tm, d), row),
                pl.BlockSpec((HALO, d), lambda i, j: (jnp.maximum(i * th - 1, 0), 0)),
                pl.BlockSpec((HALO, d), lambda i, j: (jnp.minimum((i + 1) * th, lasth), 0)),
                pl.BlockSpec((1, d), one),
                mod(sc_idx), mod(sh_idx), mod(g_idx),
                pl.BlockSpec((tm, 1), row), pl.BlockSpec((tm, 1), row),
                pl.BlockSpec((d, FF_TILE), lambda i, j: (0, j)),
                pl.BlockSpec((d, FF_TILE), lambda i, j: (0, nj + j)),
                pl.BlockSpec((3, FF_TILE), lambda i, j: (0, j)),
                pl.BlockSpec((3, FF_TILE), lambda i, j: (0, nj + j)),
                pl.BlockSpec((1, FF_TILE), lambda i, j: (0, j)),
                pl.BlockSpec((1, FF_TILE), lambda i, j: (0, nj + j)),
                pl.BlockSpec((FF_TILE, d), lambda i, j: (j, 0)),
                pl.BlockSpec((1, d), one)]
    return pl.pallas_call(
        functools.partial(_ffn_kernel, nsub=nsub, final_norm=final_norm),
        grid=(m // tm, nj),
        in_specs=in_specs,
        out_specs=pl.BlockSpec((tm, d), row),
        out_shape=jax.ShapeDtypeStruct((m, d), F32),
        scratch_shapes=[pltpu.VMEM((tm + 2 * HALO, d), BF16), pltpu.VMEM((tm, d), F32),
                        pltpu.VMEM((tm + 2 * HALO, FF_TILE), F32), pltpu.VMEM((tm + 2 * HALO, FF_TILE), F32)],
        compiler_params=_cparams(("parallel", "arbitrary")),
        name="conv_ffn",
    )(x, x, x, gain.reshape(1, d), mod_t, mod_t, mod_t, pmask, nmask, w_up, w_up, cw, cw, cb, cb, w_down,
      fin_gain.reshape(1, d))


def _rope_tables(ctx_len, seq, head_dim):
    quarter = head_dim // 4
    t = jnp.arange(seq, dtype=jnp.int32)
    inv = ROPE_BASE ** (-jnp.arange(quarter, dtype=F32) / quarter)
    ang_r = (t // GRID_W).astype(F32)[:, None] * inv
    ang_c = (t % GRID_W).astype(F32)[:, None] * inv
    cos = jnp.concatenate([jnp.cos(ang_r)] * 2 + [jnp.cos(ang_c)] * 2, axis=-1)
    sin = jnp.concatenate([-jnp.sin(ang_r), jnp.sin(ang_r), -jnp.sin(ang_c), jnp.sin(ang_c)], axis=-1)
    reps = LANES // head_dim
    cos, sin = jnp.tile(cos, (1, reps)), jnp.tile(sin, (1, reps))
    cos = jnp.concatenate([jnp.ones((ctx_len, LANES), F32), cos], axis=0)
    sin = jnp.concatenate([jnp.zeros((ctx_len, LANES), F32), sin], axis=0)
    return cos, sin


def _pad_ffn(w_up, cw, cb, w_down, ffp):
    ff = w_down.shape[0]
    pad = ffp - ff
    halves = lambda a: jnp.concatenate([jnp.pad(a[..., :ff], [(0, 0)] * (a.ndim - 1) + [(0, pad)]),
                                        jnp.pad(a[..., ff:], [(0, 0)] * (a.ndim - 1) + [(0, pad)])], axis=-1)
    return (halves(w_up).astype(BF16), halves(cw), halves(cb[None, :]),
            jnp.pad(w_down, [(0, pad), (0, 0)]).astype(BF16))


def kernel(x, c, ctx, c_ctx, w_mod, b_mod, norm1, w_in, rwkv_mu, rwkv_w0, rwkv_w2, rwkv_a0, rwkv_a2, rwkv_k_k, rwkv_k_a, rwkv_r_k, rwkv_g2, rwkv_ln_w, rwkv_ln_b, rwkv_v0, rwkv_v1, rwkv_v2, gqa_q_norm, gqa_k_norm, diff_lq1, diff_lk1, diff_lq2, diff_lk2, diff_subln, w_branch_r, w_branch_g, w_branch_d, w_out, norm2, ffn_up, ffn_conv_w, ffn_conv_b, ffn_down, final_norm):
    bsz, seq, d = x.shape
    ctx_len = ctx.shape[1]
    depth = w_in.shape[0]
    seg = ctx_len + seq
    m = bsz * seg
    seg_tiles = seg // SUB
    assert ctx_len == SUB and seq % SUB == 0 and bsz + 1 <= 8

    xs = jnp.concatenate([ctx, x], axis=1).reshape(m, d)

    pos = jnp.arange(m, dtype=jnp.int32) % seg
    pmask = ((pos != 0) & (pos != ctx_len)).astype(F32)[:, None]
    nmask = ((pos != ctx_len - 1) & (pos != seg - 1)).astype(F32)[:, None]

    tabs = _rope_tables(ctx_len, seq, G_HEAD) + _rope_tables(ctx_len, seq, DF_QK_HEAD)

    c_all = jnp.zeros((8, d), F32).at[:bsz].set(c).at[bsz].set(c_ctx)
    sub_idx = jnp.arange(m // SUB, dtype=jnp.int32)
    mod_row = jnp.where(sub_idx % seg_tiles == 0, bsz, sub_idx // seg_tiles)

    ffp = -(-ffn_down.shape[1] // FF_TILE) * FF_TILE
    tm_big = 1024 if m % 1024 == 0 else SUB
    tm_ffn = 512 if m % 512 == 0 else SUB

    v_first = None
    for i in range(depth):
        last = i == depth - 1
        mod = ada_mod(c_all, w_mod[i].astype(BF16), b_mod[i])
        mod_t = mod[mod_row][:, None, :]

        p = norm_matmul(xs, norm1[i], mod_t, 1, 0, w_in[i].astype(BF16), tm=tm_big, tn=512)

        vres = None if i == 0 else (v_first, rwkv_v0[i - 1], rwkv_v1[i - 1], rwkv_v2[i - 1])
        r_t, v_t, kn_t, lw_t, kd_t, b_t, bonus, gate_r = rwkv_prep(
            p, pmask, nmask, rwkv_mu[i], rwkv_w0[i], rwkv_w2[i], rwkv_a0[i], rwkv_a2[i],
            rwkv_k_k[i], rwkv_k_a[i], rwkv_r_k[i], rwkv_g2[i], vres)
        if i == 0:
            v_first = v_t
        mh, ml, cm, rhh, rhl, y0 = rwkv_phase1(r_t, v_t, kn_t, lw_t, kd_t, b_t)
        sfh, sfl, sbh, sbl = rwkv_phase2(mh, ml, cm, bsz=bsz, seg=seg, ctx_len=ctx_len)
        y_r = rwkv_phase3(rhh, rhl, y0, sfh, sfl, sbh, sbl, bonus, gate_r, rwkv_ln_w[i], rwkv_ln_b[i])

        qg, kg, vg, dq, dk, dv = attn_prep(p, tabs, gqa_q_norm[i], gqa_k_norm[i], seg_tiles=seg_tiles)
        y_g = gqa_attention(qg, kg, vg, bsz=bsz, seg=seg, ctx_len=ctx_len)
        lam_init = 0.8 - 0.6 * math.exp(-0.3 * i)
        y_d = diff_attention(dq, dk, dv, diff_lq1[i], diff_lk1[i], diff_lq2[i], diff_lk2[i], diff_subln[i],
                             bsz=bsz, seg=seg, ctx_len=ctx_len, lam_init=lam_init)

        mrg = gated_merge(p, y_r, y_g, y_d, w_branch_r[i].astype(BF16), w_branch_g[i].astype(BF16),
                          w_branch_d[i].astype(BF16), tm=tm_big, tn=512)
        xs = out_proj_residual(mrg, w_out[i].astype(BF16), xs, mod_t, 2, tm=tm_big, tn=512)

        w_up, cw, cb, w_down = _pad_ffn(ffn_up[i], ffn_conv_w[i], ffn_conv_b[i], ffn_down[i], ffp)
        xs = conv_ffn(xs, norm2[i], mod_t, 4, 3, 5, pmask, nmask, w_up, cw, cb, w_down, final_norm,
                      tm=tm_ffn, final_norm=last)

    return xs.reshape(bsz, seg, d)[:, ctx_len:, :]
```

```python
import functools
import math

import jax
import jax.numpy as jnp
from jax import lax
from jax.experimental import pallas as pl
from jax.experimental.pallas import tpu as pltpu

F32 = jnp.float32
BF16 = jnp.bfloat16

GRID_W = 64
ROPE_BASE = 10000.0
NORM_EPS = 1e-6
R_HEADS, R_HEAD = 12, 64
R_WIDTH = R_HEADS * R_HEAD
R_DECAY_RANK, R_ICLR_RANK, R_GATE_RANK = 64, 64, 128
R_LN_EPS = 64e-5
G_HEADS, G_KV_HEADS, G_HEAD = 6, 2, 128
G_GROUP = G_HEADS // G_KV_HEADS
G_WIDTH = G_HEADS * G_HEAD
G_KV_WIDTH = G_KV_HEADS * G_HEAD
DF_HEADS, DF_QK_HEAD = 6, 64
DF_V_HEAD = 2 * DF_QK_HEAD
DF_WIDTH = DF_HEADS * DF_V_HEAD
DF_SUBLN_EPS = 1e-5
R_COLS = 3 * R_WIDTH + R_DECAY_RANK + R_ICLR_RANK + R_GATE_RANK
G_COLS = G_WIDTH + 2 * G_KV_WIDTH
DF_COLS = 3 * DF_WIDTH

LANES = 128
SUB = 256
CHUNK = 64
HALO = 16
FF_TILE = 512
Q_SPLIT = 128
VMEM_LIMIT = 56 * 1024 * 1024

PAIR = 2 * R_HEAD
N_PAIR = R_WIDTH // PAIR
assert PAIR == LANES and CHUNK == R_HEAD


def _cparams(sem):
    return pltpu.CompilerParams(dimension_semantics=sem, vmem_limit_bytes=VMEM_LIMIT)


def _mm(a, b, prec=None):
    return lax.dot_general(a, b, (((1,), (0,)), ((), ())), precision=prec, preferred_element_type=F32)


def _mm_nt(a, b, prec=None):
    return lax.dot_general(a, b, (((1,), (1,)), ((), ())), precision=prec, preferred_element_type=F32)


def _mm1(a, b):
    return _mm(a.astype(BF16), b.astype(BF16))


def _sigmoid(x):
    return 1.0 / (1.0 + jnp.exp(-x))


def _mod_kernel(c_ref, w_ref, b_ref, o_ref):
    c = c_ref[...]
    a = (c * _sigmoid(c)).astype(BF16)
    o_ref[...] = _mm(a, w_ref[...].astype(BF16)) + b_ref[...]


def ada_mod(c_all, w_bf, b):
    rows, d = c_all.shape
    n = w_bf.shape[1]
    tn = 1024
    return pl.pallas_call(
        _mod_kernel,
        grid=(n // tn,),
        in_specs=[pl.BlockSpec((rows, d), lambda j: (0, 0)),
                  pl.BlockSpec((d, tn), lambda j: (0, j)),
                  pl.BlockSpec((1, tn), lambda j: (0, j))],
        out_specs=pl.BlockSpec((rows, tn), lambda j: (0, j)),
        out_shape=jax.ShapeDtypeStruct((rows, n), F32),
        compiler_params=_cparams(("parallel",)),
        name="ada_mod",
    )(c_all, w_bf, b.reshape(1, n))


def _modnorm(x, gain, sc, sh):
    ms = jnp.mean(x * x, axis=-1, keepdims=True)
    y = x * lax.rsqrt(ms + NORM_EPS) * gain
    return y * (1.0 + sc) + sh


def _normmm_kernel(x_ref, gain_ref, sc_ref, sh_ref, w_ref, o_ref, a_scr, *, nsub):
    @pl.when(pl.program_id(1) == 0)
    def _():
        for s in range(nsub):
            rows = pl.ds(s * SUB, SUB)
            a_scr[rows, :] = _modnorm(x_ref[rows, :], gain_ref[...], sc_ref[s], sh_ref[s]).astype(BF16)
    o_ref[...] = _mm(a_scr[...], w_ref[...])


def norm_matmul(x, gain, mod_t, sc_idx, sh_idx, w_bf, *, tm, tn):
    m, d = x.shape
    n = w_bf.shape[1]
    nsub = tm // SUB
    return pl.pallas_call(
        functools.partial(_normmm_kernel, nsub=nsub),
        grid=(m // tm, n // tn),
        in_specs=[pl.BlockSpec((tm, d), lambda i, j: (i, 0)),
                  pl.BlockSpec((1, d), lambda i, j: (0, 0)),
                  pl.BlockSpec((nsub, 1, d), lambda i, j: (i, 0, sc_idx)),
                  pl.BlockSpec((nsub, 1, d), lambda i, j: (i, 0, sh_idx)),
                  pl.BlockSpec((d, tn), lambda i, j: (0, j))],
        out_specs=pl.BlockSpec((tm, tn), lambda i, j: (i, j)),
        out_shape=jax.ShapeDtypeStruct((m, n), F32),
        scratch_shapes=[pltpu.VMEM((tm, d), BF16)],
        compiler_params=_cparams(("parallel", "arbitrary")),
        name="norm_matmul",
    )(x, gain.reshape(1, d), mod_t, mod_t, w_bf)


def _lane_partner(x, q):
    n = x.shape[-1]
    lane = lax.broadcasted_iota(jnp.int32, x.shape, x.ndim - 1)
    up = pltpu.roll(x, n - q, x.ndim - 1)
    dn = pltpu.roll(x, q, x.ndim - 1)
    return jnp.where((lane & q) == 0, up, dn)


def _group_allsum(x, group):
    r = lax.broadcasted_iota(jnp.int32, (LANES, LANES), 0)
    c = lax.broadcasted_iota(jnp.int32, (LANES, LANES), 1)
    ones_bd = jnp.where((r & -group) == (c & -group), 1.0, 0.0).astype(BF16)
    slabs = []
    for j in range(x.shape[-1] // LANES):
        xs = x[:, j * LANES:(j + 1) * LANES]
        hi = xs.astype(BF16)
        lo = (xs - hi.astype(F32)).astype(BF16)
        slabs.append(_mm(hi, ones_bd) + _mm(lo, ones_bd))
    return jnp.concatenate(slabs, axis=-1)


def _attn_prep_kernel(pg_ref, pdq_ref, pdk_ref, pdv_ref, cg_ref, sg_ref, cd_ref, sd_ref,
                      qn_ref, kn_ref, qg_ref, kg_ref, vg_ref, dq_ref, dk_ref, dv_ref):
    cg, sg = cg_ref[...], sg_ref[...]

    def head_norm_rope(x, gain, scale):
        ms = jnp.mean(x * x, axis=-1, keepdims=True)
        y = x * lax.rsqrt(ms + NORM_EPS) * gain
        y = y * cg + _lane_partner(y, G_HEAD // 4) * sg
        return y * scale

    g_scale = G_HEAD ** -0.5
    for h in range(G_HEADS):
        cols = slice(h * G_HEAD, (h + 1) * G_HEAD)
        qg_ref[:, cols] = head_norm_rope(pg_ref[:, cols], qn_ref[...], g_scale).astype(BF16)
    for h in range(G_KV_HEADS):
        cols = slice(h * G_HEAD, (h + 1) * G_HEAD)
        src = slice(G_WIDTH + h * G_HEAD, G_WIDTH + (h + 1) * G_HEAD)
        kg_ref[:, cols] = head_norm_rope(pg_ref[:, src], kn_ref[...], 1.0).astype(BF16)
    vg_ref[...] = pg_ref[:, G_WIDTH + G_KV_WIDTH:].astype(BF16)

    cd, sd = cd_ref[...], sd_ref[...]
    d_scale = DF_QK_HEAD ** -0.5
    for h in range(DF_HEADS):
        cols = slice(h * DF_V_HEAD, (h + 1) * DF_V_HEAD)
        xq = pdq_ref[:, cols]
        dq_ref[:, cols] = ((xq * cd + _lane_partner(xq, DF_QK_HEAD // 4) * sd) * d_scale).astype(BF16)
        xk = pdk_ref[:, cols]
        dk_ref[:, cols] = (xk * cd + _lane_partner(xk, DF_QK_HEAD // 4) * sd).astype(BF16)
    dv_ref[...] = pdv_ref[...].astype(BF16)


def attn_prep(p, tabs, q_gain, k_gain, *, seg_tiles):
    m = p.shape[0]
    cg, sg, cd, sd = tabs
    g_blk = R_COLS // G_COLS
    d_blk = (R_COLS + G_COLS) // DF_WIDTH
    assert g_blk * G_COLS == R_COLS and d_blk * DF_WIDTH == R_COLS + G_COLS
    row = lambda i: (i, 0)
    tab = lambda i: (i % seg_tiles, 0)
    one = lambda i: (0, 0)
    outs = [jax.ShapeDtypeStruct((m, w), BF16) for w in (G_WIDTH, G_KV_WIDTH, G_KV_WIDTH, DF_WIDTH, DF_WIDTH, DF_WIDTH)]
    return pl.pallas_call(
        _attn_prep_kernel,
        grid=(m // SUB,),
        in_specs=[pl.BlockSpec((SUB, G_COLS), lambda i: (i, g_blk)),
                  pl.BlockSpec((SUB, DF_WIDTH), lambda i: (i, d_blk)),
                  pl.BlockSpec((SUB, DF_WIDTH), lambda i: (i, d_blk + 1)),
                  pl.BlockSpec((SUB, DF_WIDTH), lambda i: (i, d_blk + 2)),
                  pl.BlockSpec((SUB, LANES), tab), pl.BlockSpec((SUB, LANES), tab),
                  pl.BlockSpec((SUB, LANES), tab), pl.BlockSpec((SUB, LANES), tab),
                  pl.BlockSpec((1, G_HEAD), one), pl.BlockSpec((1, G_HEAD), one)],
        out_specs=[pl.BlockSpec((SUB, o.shape[1]), row) for o in outs],
        out_shape=outs,
        compiler_params=_cparams(("parallel",)),
        name="attn_prep",
    )(p, p, p, p, cg, sg, cd, sd, q_gain.reshape(1, G_HEAD), k_gain.reshape(1, G_HEAD))


def _gqa_kernel(q_ref, k_ref, v_ref, o_ref, *, ctx_len):
    blocks = [slice(i * Q_SPLIT, (i + 1) * Q_SPLIT) for i in range(q_ref.shape[0] // Q_SPLIT)]

    def attend(klen):
        k, v = k_ref[0:klen, :], v_ref[0:klen, :]
        s = [_mm_nt(q_ref[rw, :], k) for rw in blocks]
        e = [jnp.exp(x - jnp.max(x, axis=-1, keepdims=True)) for x in s]
        l = [jnp.sum(x, axis=-1, keepdims=True) for x in e]
        for rw, x, y in zip(blocks, e, l):
            o_ref[rw, :] = (_mm(x.astype(BF16), v) / y).astype(o_ref.dtype)

    is_ctx = pl.program_id(2) == 0
    pl.when(is_ctx)(lambda: attend(ctx_len))
    pl.when(jnp.logical_not(is_ctx))(lambda: attend(k_ref.shape[0]))


def gqa_attention(qg, kg, vg, *, bsz, seg, ctx_len):
    m = qg.shape[0]
    seg_tiles = seg // SUB
    k3 = kg.reshape(bsz, seg, G_KV_WIDTH)
    v3 = vg.reshape(bsz, seg, G_KV_WIDTH)
    qo = lambda b, h, t, g: (b * seg_tiles + t, h * G_GROUP + g)
    kv = lambda b, h, t, g: (b, 0, h)
    return pl.pallas_call(
        functools.partial(_gqa_kernel, ctx_len=ctx_len),
        grid=(bsz, G_KV_HEADS, seg_tiles, G_GROUP),
        in_specs=[pl.BlockSpec((SUB, G_HEAD), qo),
                  pl.BlockSpec((None, seg, G_HEAD), kv),
                  pl.BlockSpec((None, seg, G_HEAD), kv)],
        out_specs=pl.BlockSpec((SUB, G_HEAD), qo),
        out_shape=jax.ShapeDtypeStruct((m, G_WIDTH), BF16),
        compiler_params=_cparams(("parallel", "parallel", "parallel", "parallel")),
        name="gqa_attention",
    )(qg, k3, v3)


def _diff_kernel(q_ref, k_ref, v_ref, lq1_ref, lk1_ref, lq2_ref, lk2_ref, gain_ref, o_ref, *, ctx_len, lam_init):
    lane = lax.broadcasted_iota(jnp.int32, (Q_SPLIT, DF_V_HEAD), 1)
    zero = jnp.zeros((Q_SPLIT, DF_V_HEAD), q_ref.dtype)
    lam = (jnp.exp(jnp.sum(lq1_ref[...] * lk1_ref[...], axis=-1, keepdims=True))
           - jnp.exp(jnp.sum(lq2_ref[...] * lk2_ref[...], axis=-1, keepdims=True)) + lam_init)
    blocks = [slice(i * Q_SPLIT, (i + 1) * Q_SPLIT) for i in range(q_ref.shape[0] // Q_SPLIT)]

    def attend(klen):
        k, v = k_ref[0:klen, :], v_ref[0:klen, :]
        s0 = [_mm_nt(jnp.where(lane < DF_QK_HEAD, q_ref[rw, :], zero), k) for rw in blocks]
        s1 = [_mm_nt(jnp.where(lane >= DF_QK_HEAD, q_ref[rw, :], zero), k) for rw in blocks]
        e0 = [jnp.exp(x - jnp.max(x, axis=-1, keepdims=True)) for x in s0]
        e1 = [jnp.exp(x - jnp.max(x, axis=-1, keepdims=True)) for x in s1]
        c0 = [1.0 / jnp.sum(x, axis=-1, keepdims=True) for x in e0]
        c1 = [lam / jnp.sum(x, axis=-1, keepdims=True) for x in e1]
        w = [(x0 * y0 - x1 * y1).astype(BF16) for x0, y0, x1, y1 in zip(e0, c0, e1, c1)]
        for rw, x in zip(blocks, w):
            o = _mm(x, v)
            ms = jnp.mean(o * o, axis=-1, keepdims=True)
            o = o * lax.rsqrt(ms + DF_SUBLN_EPS) * gain_ref[...] * (1.0 - lam_init)
            o_ref[rw, :] = o.astype(o_ref.dtype)

    is_ctx = pl.program_id(2) == 0
    pl.when(is_ctx)(lambda: attend(ctx_len))
    pl.when(jnp.logical_not(is_ctx))(lambda: attend(k_ref.shape[0]))


def diff_attention(dq, dk, dv, lq1, lk1, lq2, lk2, gain, *, bsz, seg, ctx_len, lam_init):
    m = dq.shape[0]
    seg_tiles = seg // SUB
    k3 = dk.reshape(bsz, seg, DF_WIDTH)
    v3 = dv.reshape(bsz, seg, DF_WIDTH)
    qo = lambda b, h, t: (b * seg_tiles + t, h)
    kv = lambda b, h, t: (b, 0, h)
    one = lambda b, h, t: (0, 0)
    vec = pl.BlockSpec((1, DF_QK_HEAD), one)
    return pl.pallas_call(
        functools.partial(_diff_kernel, ctx_len=ctx_len, lam_init=lam_init),
        grid=(bsz, DF_HEADS, seg_tiles),
        in_specs=[pl.BlockSpec((SUB, DF_V_HEAD), qo),
                  pl.BlockSpec((None, seg, DF_V_HEAD), kv),
                  pl.BlockSpec((None, seg, DF_V_HEAD), kv),
                  vec, vec, vec, vec, pl.BlockSpec((1, DF_V_HEAD), one)],
        out_specs=pl.BlockSpec((SUB, DF_V_HEAD), qo),
        out_shape=jax.ShapeDtypeStruct((m, DF_WIDTH), BF16),
        compiler_params=_cparams(("parallel", "parallel", "parallel")),
        name="diff_attention",
    )(dq, k3, v3, lq1.reshape(1, -1), lk1.reshape(1, -1), lq2.reshape(1, -1), lk2.reshape(1, -1),
      gain.reshape(1, -1))


def _rwkv_prep_kernel(*refs, has_vres):
    (pr_ref, prev_ref, next_ref, pm_ref, nm_ref, mu_ref, w0_ref, w2_ref, a0_ref, a2_ref,
     kk_ref, ka_ref, rk_ref, g2_ref) = refs[:14]
    refs = refs[14:]
    if has_vres:
        vf_ref, v0_ref, v1_ref, v2_ref = refs[:4]
        refs = refs[4:]
    r_out, v_out, kn_out, lw_out, kd_out, b_out, bonus_out, g_out = refs

    p = pr_ref[...]
    rows = lax.broadcasted_iota(jnp.int32, (SUB, 1), 0)
    prev = jnp.where(rows == 0, prev_ref[7:8, :], pltpu.roll(p, 1, 0)) * pm_ref[...]
    nxt = jnp.where(rows == SUB - 1, next_ref[0:1, :], pltpu.roll(p, SUB - 1, 0)) * nm_ref[...]
    xs = p + mu_ref[...] * (0.5 * (prev + nxt) - p)

    r = xs[:, 0:R_WIDTH]
    k = xs[:, R_WIDTH:2 * R_WIDTH]
    v = xs[:, 2 * R_WIDTH:3 * R_WIDTH]
    o = 3 * R_WIDTH
    w_lo = xs[:, o:o + R_DECAY_RANK]
    a_lo = xs[:, o + R_DECAY_RANK:o + R_DECAY_RANK + R_ICLR_RANK]
    g_lo = xs[:, o + R_DECAY_RANK + R_ICLR_RANK:]

    if has_vres:
        mix = _sigmoid(v0_ref[...] + _mm1(_mm1(v, v1_ref[...]), v2_ref[...]))
        v = v + (vf_ref[...] - v) * mix

    kk = k * kk_ref[...]
    nrm = jnp.sqrt(_group_allsum(kk * kk, R_HEAD))
    kk = kk / jnp.maximum(nrm, 1e-12)
    tw = jnp.tanh(w_lo)
    ksum = jnp.zeros_like(k)
    dec_scale = math.exp(-0.5)
    for d in range(2):
        z = w0_ref[d:d + 1, :] + _mm1(tw, w2_ref[d])
        lw_out[d] = -dec_scale * _sigmoid(z)
        a = _sigmoid(a0_ref[d:d + 1, :] + _mm1(a_lo, a2_ref[d]))
        k_d = k * (1.0 + (a - 1.0) * ka_ref[...])
        kd_out[d] = k_d
        b_out[d] = kk * a
        ksum = ksum + k_d
    r_out[...] = r
    v_out[...] = v
    kn_out[...] = kk
    bonus_out[...] = _group_allsum(r * ksum * rk_ref[...], R_HEAD) * v
    g_out[...] = _mm1(_sigmoid(g_lo), g2_ref[...])


def rwkv_prep(p, pmask, nmask, mu, w0, w2, a0, a2, k_k, k_a, r_k, g2, vres):
    m = p.shape[0]
    nt = m // SUB
    sub8 = SUB // 8
    last8 = m // 8 - 1
    one2 = lambda i: (0, 0)
    one3 = lambda i: (0, 0, 0)
    w = R_WIDTH
    in_specs = [pl.BlockSpec((SUB, R_COLS), lambda i: (i, 0)),
                pl.BlockSpec((8, R_COLS), lambda i: (jnp.maximum(i * sub8 - 1, 0), 0)),
                pl.BlockSpec((8, R_COLS), lambda i: (jnp.minimum((i + 1) * sub8, last8), 0)),
                pl.BlockSpec((SUB, 1), lambda i: (i, 0)),
                pl.BlockSpec((SUB, 1), lambda i: (i, 0)),
                pl.BlockSpec((1, R_COLS), one2),
                pl.BlockSpec((2, w), one2),
                pl.BlockSpec((2, R_DECAY_RANK, w), one3),
                pl.BlockSpec((2, w), one2),
                pl.BlockSpec((2, R_ICLR_RANK, w), one3),
                pl.BlockSpec((1, w), one2), pl.BlockSpec((1, w), one2), pl.BlockSpec((1, w), one2),
                pl.BlockSpec((R_GATE_RANK, w), one2)]
    args = [p, p, p, pmask, nmask, mu.reshape(1, -1), w0, w2, a0, a2, k_k.reshape(1, w), k_a.reshape(1, w),
            r_k.reshape(1, w), g2]
    if vres is not None:
        v_first, v0, v1, v2 = vres
        in_specs += [pl.BlockSpec((SUB, w), lambda i: (i, 0)), pl.BlockSpec((1, w), one2),
                     pl.BlockSpec(v1.shape, one2), pl.BlockSpec(v2.shape, one2)]
        args += [v_first, v0.reshape(1, w), v1, v2]
    ld = jax.ShapeDtypeStruct((m, w), F32)
    ld2 = jax.ShapeDtypeStruct((2, m, w), F32)
    ld_spec = pl.BlockSpec((SUB, w), lambda i: (i, 0))
    ld2_spec = pl.BlockSpec((2, SUB, w), lambda i: (0, i, 0))
    return pl.pallas_call(
        functools.partial(_rwkv_prep_kernel, has_vres=vres is not None),
        grid=(nt,),
        in_specs=in_specs,
        out_specs=[ld_spec, ld_spec, ld_spec, ld2_spec, ld2_spec, ld2_spec, ld_spec, ld_spec],
        out_shape=[ld, ld, ld, ld2, ld2, ld2, ld, ld],
        compiler_params=_cparams(("parallel",)),
        name="rwkv_prep",
    )(*args)


def _split2(a):
    hi = a.astype(BF16)
    lo = (a - hi.astype(F32)).astype(BF16)
    return hi, lo


def _split3(a):
    h1 = a.astype(BF16)
    r1 = a - h1.astype(F32)
    h2 = r1.astype(BF16)
    return h1, h2, (r1 - h2.astype(F32)).astype(BF16)


def _mm3(a, b, nt=False):
    f = _mm_nt if nt else _mm
    return f(a[0], b[0]) + (f(a[0], b[1]) + f(a[1], b[0]))


def _pair_consts(sign):
    t = lax.broadcasted_iota(jnp.int32, (CHUNK, PAIR), 0)
    lane = lax.broadcasted_iota(jnp.int32, (CHUNK, PAIR), 1)
    ahead = (t - (lane & (R_HEAD - 1))) * sign
    one, zero = jnp.ones((CHUNK, PAIR), F32), jnp.zeros((CHUNK, PAIR), F32)
    head0 = lane < R_HEAD
    t3 = lax.broadcasted_iota(jnp.int32, (CHUNK, 3 * CHUNK), 0)
    i3 = lax.broadcasted_iota(jnp.int32, (CHUNK, 3 * CHUNK), 1) & (CHUNK - 1)
    r2 = lax.broadcasted_iota(jnp.int32, (PAIR, PAIR), 0)
    l2 = lax.broadcasted_iota(jnp.int32, (PAIR, PAIR), 1)
    return dict(
        strict=ahead > 0, incl=ahead >= 0, eye=jnp.where(ahead == 0, one, zero), head0=head0,
        m0=jnp.where(head0, one, zero).astype(BF16), m1=jnp.where(head0, zero, one).astype(BF16),
        tri3=jnp.where((t3 - i3) * sign >= 0, 1.0, 0.0).astype(BF16),
        same_head=(r2 >= R_HEAD) == (l2 >= R_HEAD), eye2=(r2 == l2).astype(F32))


def _bd(x, c):
    return tuple(jnp.concatenate([u * c["m0"], u * c["m1"]], axis=0) for u in x)


def _chunks_affine(r, v, kk, lw, k, b, cs):
    each = lambda f, *xs: [f(*a) for a in zip(*xs)]
    c = cs[0]
    zero = jnp.zeros((CHUNK, PAIR), F32)
    bd = lambda x: _bd(x, c)
    mm3 = lambda x, y: _mm3(x, y)
    mm3_nt = lambda x, y: _mm3(x, y, nt=True)
    mm1 = lambda x, y: _mm(x[0], y[0])
    hi = lambda x: (x.astype(BF16),)
    top = lambda x: x[:CHUNK]
    bot = lambda x: x[CHUNK:]
    stack = lambda x, y: jnp.concatenate([x, y], axis=0)

    cum = each(lambda d, x: _mm(d["tri3"], jnp.concatenate(_split3(x), axis=0)), cs, lw)
    tot = each(lambda x: jnp.sum(x, axis=0, keepdims=True), lw)
    e_out = each(lambda x: jnp.exp(-x), cum)
    at = each(lambda kn, cu, x: -kn * jnp.exp(cu - x), kk, cum, lw)
    rt = each(lambda x, cu: x * jnp.exp(cu), r, cum)
    e_end = each(lambda t_, cu: jnp.exp(t_ - cu), tot, cum)
    v_bd = each(lambda x: bd(hi(x)), v)
    lhs = each(lambda x, y: _split2(stack(x, y)), at, rt)
    xb = each(mm3_nt, lhs, each(lambda x, e: bd(_split2(x * e)), b, e_out))
    xk = each(lambda x, y: _mm_nt(x[0], y[0]), lhs, each(lambda x, e: bd(hi(x * e)), k, e_out))
    a_ab = each(lambda d, x: jnp.where(d["strict"], top(x), zero), cs, xb)
    a_rb = each(lambda d, x: hi(jnp.where(d["incl"], bot(x), zero)), cs, xb)
    a_ak = each(lambda d, x: hi(jnp.where(d["strict"], top(x), zero)), cs, xk)
    a_rk = each(lambda d, x: hi(jnp.where(d["incl"], bot(x), zero)), cs, xk)
    t = each(lambda x: c["eye"] + x, a_ab)
    a_s = each(_split2, a_ab)
    pw = each(mm3, a_s, each(bd, a_s))
    for _ in range(int(math.log2(CHUNK)) - 2):
        x = each(mm3, each(lambda x, y: _split2(stack(x, y)), t, pw), each(lambda y: bd(_split2(y)), pw))
        t = each(lambda t_, x_: t_ + top(x_), t, x)
        pw = each(bot, x)
    t = each(lambda t_, p_: t_ + mm3(_split2(t_), bd(_split2(p_))), t, pw)
    t_s = each(_split2, t)
    ah = each(mm3, t_s, each(lambda x: bd(_split2(x)), at))
    x1 = each(mm1, a_ak, v_bd)
    u0 = each(mm3, t_s, each(lambda x: bd(_split2(x)), x1))
    rh = each(lambda r_, a_, h_: r_ + mm1(a_, bd(hi(h_))), rt, a_rb, ah)
    y0 = each(lambda a_, u_, k_, v_: mm1(a_, bd(hi(u_))) + mm1(k_, v_), a_rb, u0, a_rk, v_bd)
    bb = each(lambda x, e: _split2(x * e), b, e_end)
    kb = each(lambda x, e: _split2(x * e), k, e_end)
    m_full = each(lambda h_, b_: mm3(_split2(h_.T), b_), ah, bb)
    m_bd = each(lambda x, t_: jnp.where(c["same_head"], x, jnp.zeros_like(x)) + c["eye2"] * jnp.exp(t_), m_full, tot)
    c_full = each(lambda u_, b_, v_, k_: mm3(_split2(u_.T), b_) + mm3(_split2(v_.T), k_), u0, bb, v, kb)
    cm = each(lambda x: jnp.where(c["head0"], top(x), bot(x)), c_full)
    return m_bd, cm, rh, y0


def _rwkv_phase1_kernel(r_ref, v_ref, kn_ref, lw_ref, kd_ref, b_ref,
                        mh_out, ml_out, c_out, rh_out, y0_out):
    n = SUB // CHUNK
    rows = [slice(ci * CHUNK, (ci + 1) * CHUNK) for ci in range(n)]
    shared = lambda ref: [ref[rw, :] for rw in rows] * 2
    per_dir = lambda ref: [ref[d, rw, :] for d in range(2) for rw in rows]
    cs = [_pair_consts(1)] * n + [_pair_consts(-1)] * n
    m_bd, cm, rh, y0 = _chunks_affine(shared(r_ref), shared(v_ref), shared(kn_ref), per_dir(lw_ref),
                                      per_dir(kd_ref), per_dir(b_ref), cs)
    for d in range(2):
        for ci in range(n):
            i = d * n + ci
            mrows = slice(ci * PAIR, (ci + 1) * PAIR)
            mh_out[d, mrows, :], ml_out[d, mrows, :] = _split2(m_bd[i])
            c_out[d, rows[ci], :] = cm[i]
            rh_out[d, rows[ci], :] = rh[i].astype(BF16)
    for ci in range(n):
        y0_out[rows[ci], :] = y0[ci] + y0[n + ci]


def rwkv_phase1(r, v, kn, lw, kd, b):
    m, w = r.shape
    cpt = SUB // CHUNK
    s_spec = pl.BlockSpec((SUB, PAIR), lambda t, pr: (t, pr))
    d_spec = pl.BlockSpec((2, SUB, PAIR), lambda t, pr: (0, t, pr))
    m_spec = pl.BlockSpec((2, None, cpt * PAIR, PAIR), lambda t, pr: (0, pr, t, 0))
    m_shape = jax.ShapeDtypeStruct((2, N_PAIR, (m // CHUNK) * PAIR, PAIR), BF16)
    tok = lambda dt: jax.ShapeDtypeStruct((2, m, w), dt)
    return pl.pallas_call(
        _rwkv_phase1_kernel,
        grid=(m // SUB, N_PAIR),
        in_specs=[s_spec, s_spec, s_spec, d_spec, d_spec, d_spec],
        out_specs=[m_spec, m_spec, d_spec, d_spec, s_spec],
        out_shape=[m_shape, m_shape, tok(F32), tok(BF16), jax.ShapeDtypeStruct((m, w), F32)],
        compiler_params=_cparams(("parallel", "parallel")),
        name="rwkv_phase1",
    )(r, v, kn, lw, kd, b)


def _rwkv_phase2_kernel(mhf_ref, mlf_ref, cf_ref, mhb_ref, mlb_ref, cb_ref,
                        sf_out, sb_out, s_scr):
    @pl.when(pl.program_id(0) == 0)
    def _():
        s_scr[...] = jnp.zeros_like(s_scr)

    bsz = cf_ref.shape[0]
    dirs = ((mhf_ref, mlf_ref, cf_ref, sf_out), (mhb_ref, mlb_ref, cb_ref, sb_out))
    for d, (mh_ref, ml_ref, c_ref, s_out) in enumerate(dirs):
        for bi in range(bsz):
            for pr in range(N_PAIR):
                cols = slice(pr * PAIR, (pr + 1) * PAIR)
                s = _split2(s_scr[d, bi, :, cols])
                s_out[bi, :, cols] = s[0]
                s_scr[d, bi, :, cols] = _mm3(s, (mh_ref[pr, bi], ml_ref[pr, bi])) + c_ref[bi, :, cols]


def rwkv_phase2(mh, ml, cm, *, bsz, seg, ctx_len):
    w = cm.shape[-1]
    nchunk = seg // CHUNK
    nctx = ctx_len // CHUNK
    mh5 = mh.reshape(2, N_PAIR, bsz, nchunk * PAIR, PAIR)
    ml5 = ml.reshape(2, N_PAIR, bsz, nchunk * PAIR, PAIR)
    c4 = cm.reshape(2, bsz, seg, w)
    fwd = lambda s: s
    bwd = lambda s: jnp.where(s < nctx, nctx - 1 - s, nchunk - 1 + nctx - s)
    mblk = (None, N_PAIR, bsz, PAIR, PAIR)
    cblk = (None, bsz, CHUNK, w)
    m_spec = lambda d, order: pl.BlockSpec(mblk, lambda s: (d, 0, 0, order(s), 0))
    c_spec = lambda d, order: pl.BlockSpec(cblk, lambda s: (d, 0, order(s), 0))
    o_spec = lambda order: pl.BlockSpec((bsz, CHUNK, w), lambda s: (0, order(s), 0))
    out = jax.ShapeDtypeStruct((bsz, seg, w), BF16)
    outs = pl.pallas_call(
        _rwkv_phase2_kernel,
        grid=(nchunk,),
        in_specs=[m_spec(0, fwd), m_spec(0, fwd), c_spec(0, fwd), m_spec(1, bwd), m_spec(1, bwd), c_spec(1, bwd)],
        out_specs=[o_spec(fwd), o_spec(bwd)],
        out_shape=[out] * 2,
        scratch_shapes=[pltpu.VMEM((2, bsz, R_HEAD, w), F32)],
        compiler_params=_cparams(("arbitrary",)),
        name="rwkv_phase2",
    )(mh5, ml5, c4, mh5, ml5, c4)
    return [o.reshape(bsz * seg, w) for o in outs]


def _rwkv_phase3_kernel(rh_ref, y0_ref, sf_ref, sb_ref,
                        bonus_ref, g_ref, lnw_ref, lnb_ref, o_ref, y_scr):
    lane = lax.broadcasted_iota(jnp.int32, (CHUNK, PAIR), 1)
    one, zero = jnp.ones((CHUNK, PAIR), F32), jnp.zeros((CHUNK, PAIR), F32)
    c = dict(m0=jnp.where(lane < R_HEAD, one, zero).astype(BF16),
             m1=jnp.where(lane < R_HEAD, zero, one).astype(BF16))

    def body(ci, carry):
        rows = pl.ds(pl.multiple_of(ci * CHUNK, CHUNK), CHUNK)
        for pr in range(N_PAIR):
            cols = slice(pr * PAIR, (pr + 1) * PAIR)
            y = y0_ref[rows, cols]
            for d, s_ref in enumerate((sf_ref, sb_ref)):
                y = y + _mm_nt(rh_ref[d, rows, cols], _bd((s_ref[rows, cols],), c)[0])
            y_scr[rows, cols] = y
        return carry

    lax.fori_loop(0, SUB // CHUNK, body, 0)
    y = y_scr[...]
    inv_n = 1.0 / R_HEAD
    yc = y - _group_allsum(y, R_HEAD) * inv_n
    var = _group_allsum(yc * yc, R_HEAD) * inv_n
    yn = yc * lax.rsqrt(var + R_LN_EPS) * lnw_ref[...] + lnb_ref[...]
    o_ref[...] = ((yn + bonus_ref[...]) * g_ref[...]).astype(o_ref.dtype)


def rwkv_phase3(rh, y0, sf, sb, bonus, g, ln_w, ln_b):
    m, w = y0.shape
    d_spec = pl.BlockSpec((2, SUB, w), lambda t: (0, t, 0))
    ld_spec = pl.BlockSpec((SUB, w), lambda t: (t, 0))
    one = pl.BlockSpec((1, w), lambda t: (0, 0))
    return pl.pallas_call(
        _rwkv_phase3_kernel,
        grid=(m // SUB,),
        in_specs=[d_spec, ld_spec, ld_spec, ld_spec, ld_spec, ld_spec, one, one],
        out_specs=ld_spec,
        out_shape=jax.ShapeDtypeStruct((m, w), BF16),
        scratch_shapes=[pltpu.VMEM((SUB, w), F32)],
        compiler_params=_cparams(("parallel",)),
        name="rwkv_phase3",
    )(rh, y0, sf, sb, bonus, g, ln_w.reshape(1, w), ln_b.reshape(1, w))


def _merge_kernel(yr_ref, yg_ref, yd_ref, wr_ref, wg_ref, wd_ref, gr_ref, gg_ref, gd_ref, o_ref):
    acc = _sigmoid(gr_ref[...]) * _mm(yr_ref[...], wr_ref[...])
    acc = acc + _sigmoid(gg_ref[...]) * _mm(yg_ref[...], wg_ref[...])
    acc = acc + _sigmoid(gd_ref[...]) * _mm(yd_ref[...], wd_ref[...])
    o_ref[...] = acc.astype(o_ref.dtype)


def gated_merge(p, y_r, y_g, y_d, wr, wg, wd, *, tm, tn):
    m = p.shape[0]
    d = wr.shape[1]
    gate0 = (R_COLS + G_COLS + DF_COLS) // tn
    assert gate0 * tn == R_COLS + G_COLS + DF_COLS and d % tn == 0
    nb = d // tn
    y_spec = lambda w: pl.BlockSpec((tm, w), lambda i, j: (i, 0))
    w_spec = lambda w: pl.BlockSpec((w, tn), lambda i, j: (0, j))
    g_spec = lambda br: pl.BlockSpec((tm, tn), lambda i, j: (i, gate0 + br * nb + j))
    return pl.pallas_call(
        _merge_kernel,
        grid=(m // tm, nb),
        in_specs=[y_spec(R_WIDTH), y_spec(G_WIDTH), y_spec(DF_WIDTH),
                  w_spec(R_WIDTH), w_spec(G_WIDTH), w_spec(DF_WIDTH),
                  g_spec(0), g_spec(1), g_spec(2)],
        out_specs=pl.BlockSpec((tm, tn), lambda i, j: (i, j)),
        out_shape=jax.ShapeDtypeStruct((m, d), BF16),
        compiler_params=_cparams(("parallel", "parallel")),
        name="gated_merge",
    )(y_r, y_g, y_d, wr, wg, wd, p, p, p)


def _outproj_kernel(a_ref, w_ref, x_ref, g_ref, o_ref, *, nsub):
    y = _mm(a_ref[...], w_ref[...])
    for s in range(nsub):
        rows = pl.ds(s * SUB, SUB)
        o_ref[rows, :] = x_ref[rows, :] + g_ref[s] * y[s * SUB:(s + 1) * SUB, :]


def out_proj_residual(a, w_bf, x, mod_t, g_idx, *, tm, tn):
    m, d = x.shape
    nsub = tm // SUB
    gpb = d // tn
    return pl.pallas_call(
        functools.partial(_outproj_kernel, nsub=nsub),
        grid=(m // tm, d // tn),
        in_specs=[pl.BlockSpec((tm, a.shape[1]), lambda i, j: (i, 0)),
                  pl.BlockSpec((a.shape[1], tn), lambda i, j: (0, j)),
                  pl.BlockSpec((tm, tn), lambda i, j: (i, j)),
                  pl.BlockSpec((nsub, 1, tn), lambda i, j: (i, 0, g_idx * gpb + j))],
        out_specs=pl.BlockSpec((tm, tn), lambda i, j: (i, j)),
        out_shape=jax.ShapeDtypeStruct((m, d), F32),
        compiler_params=_cparams(("parallel", "parallel")),
        name="out_proj_residual",
    )(a, w_bf, x, mod_t)


def _ffn_kernel(x_ref, xp_ref, xn_ref, gain_ref, sc_ref, sh_ref, g_ref, pm_ref, nm_ref,
                wv_ref, wg_ref, cwv_ref, cwg_ref, cbv_ref, cbg_ref, wd_ref, fin_ref,
                o_ref, h_scr, acc_scr, uv_scr, ug_scr, *, nsub, final_norm):
    j = pl.program_id(1)
    tm = nsub * SUB

    @pl.when(j == 0)
    def _():
        gain = gain_ref[...]
        h_scr[0:HALO, :] = _modnorm(xp_ref[...], gain, sc_ref[0], sh_ref[0]).astype(BF16)
        for s in range(nsub):
            h_scr[pl.ds(HALO + s * SUB, SUB), :] = _modnorm(
                x_ref[pl.ds(s * SUB, SUB), :], gain, sc_ref[s], sh_ref[s]).astype(BF16)
        h_scr[HALO + tm:, :] = _modnorm(xn_ref[...], gain, sc_ref[nsub - 1], sh_ref[nsub - 1]).astype(BF16)
        acc_scr[...] = jnp.zeros_like(acc_scr)

    h = h_scr[...]
    uv_scr[...] = _mm(h, wv_ref[...])
    ug_scr[...] = _mm(h, wg_ref[...])
    pm, nm = pm_ref[...], nm_ref[...]

    def conv(u_scr, cw_ref, cb_ref):
        prev = u_scr[pl.ds(HALO - 1, tm), :] * pm
        cur = u_scr[pl.ds(HALO, tm), :]
        nxt = u_scr[pl.ds(HALO + 1, tm), :] * nm
        return prev * cw_ref[0:1, :] + cur * cw_ref[1:2, :] + nxt * cw_ref[2:3, :] + cb_ref[...]

    val = conv(uv_scr, cwv_ref, cbv_ref)
    gate = conv(ug_scr, cwg_ref, cbg_ref)
    act = (val * (gate * _sigmoid(gate))).astype(BF16)
    acc_scr[...] += _mm(act, wd_ref[...])

    @pl.when(j == pl.num_programs(1) - 1)
    def _():
        for s in range(nsub):
            rows = pl.ds(s * SUB, SUB)
            y = x_ref[rows, :] + g_ref[s] * acc_scr[rows, :]
            if final_norm:
                ms = jnp.mean(y * y, axis=-1, keepdims=True)
                y = y * lax.rsqrt(ms + NORM_EPS) * fin_ref[...]
            o_ref[rows, :] = y


def conv_ffn(x, gain, mod_t, sc_idx, sh_idx, g_idx, pmask, nmask, w_up, cw, cb, w_down, fin_gain,
             *, tm, final_norm):
    m, d = x.shape
    ffp = w_down.shape[0]
    nj = ffp // FF_TILE
    nsub = tm // SUB
    th = tm // HALO
    lasth = m // HALO - 1
    row = lambda i, j: (i, 0)
    one = lambda i, j: (0, 0)
    mod = lambda idx: pl.BlockSpec((nsub, 1, d), lambda i, j: (i, 0, idx))
    in_specs = [pl.BlockSpec((tm, d), row),
                pl.BlockSpec((HALO, d), lambda i, j: (jnp.maximum(i * th - 1, 0), 0)),
                pl.BlockSpec((HALO, d), lambda i, j: (jnp.minimum((i + 1) * th, lasth), 0)),
                pl.BlockSpec((1, d), one),
                mod(sc_idx), mod(sh_idx), mod(g_idx),
                pl.BlockSpec((tm, 1), row), pl.BlockSpec((tm, 1), row),
                pl.BlockSpec((d, FF_TILE), lambda i, j: (0, j)),
                pl.BlockSpec((d, FF_TILE), lambda i, j: (0, nj + j)),
                pl.BlockSpec((3, FF_TILE), lambda i, j: (0, j)),
                pl.BlockSpec((3, FF_TILE), lambda i, j: (0, nj + j)),
                pl.BlockSpec((1, FF_TILE), lambda i, j: (0, j)),
                pl.BlockSpec((1, FF_TILE), lambda i, j: (0, nj + j)),
                pl.BlockSpec((FF_TILE, d), lambda i, j: (j, 0)),
                pl.BlockSpec((1, d), one)]
    return pl.pallas_call(
        functools.partial(_ffn_kernel, nsub=nsub, final_norm=final_norm),
        grid=(m // tm, nj),
        in_specs=in_specs,
        out_specs=pl.BlockSpec((tm, d), row),
        out_shape=jax.ShapeDtypeStruct((m, d), F32),
        scratch_shapes=[pltpu.VMEM((tm + 2 * HALO, d), BF16), pltpu.VMEM((tm, d), F32),
                        pltpu.VMEM((tm + 2 * HALO, FF_TILE), F32), pltpu.VMEM((tm + 2 * HALO, FF_TILE), F32)],
        compiler_params=_cparams(("parallel", "arbitrary")),
        name="conv_ffn",
    )(x, x, x, gain.reshape(1, d), mod_t, mod_t, mod_t, pmask, nmask, w_up, w_up, cw, cw, cb, cb, w_down,
      fin_gain.reshape(1, d))


def _rope_tables(ctx_len, seq, head_dim):
    quarter = head_dim // 4
    t = jnp.arange(seq, dtype=jnp.int32)
    inv = ROPE_BASE ** (-jnp.arange(quarter, dtype=F32) / quarter)
    ang_r = (t // GRID_W).astype(F32)[:, None] * inv
    ang_c = (t % GRID_W).astype(F32)[:, None] * inv
    cos = jnp.concatenate([jnp.cos(ang_r)] * 2 + [jnp.cos(ang_c)] * 2, axis=-1)
    sin = jnp.concatenate([-jnp.sin(ang_r), jnp.sin(ang_r), -jnp.sin(ang_c), jnp.sin(ang_c)], axis=-1)
    reps = LANES // head_dim
    cos, sin = jnp.tile(cos, (1, reps)), jnp.tile(sin, (1, reps))
    cos = jnp.concatenate([jnp.ones((ctx_len, LANES), F32), cos], axis=0)
    sin = jnp.concatenate([jnp.zeros((ctx_len, LANES), F32), sin], axis=0)
    return cos, sin


def _pad_ffn(w_up, cw, cb, w_down, ffp):
    ff = w_down.shape[0]
    pad = ffp - ff
    halves = lambda a: jnp.concatenate([jnp.pad(a[..., :ff], [(0, 0)] * (a.ndim - 1) + [(0, pad)]),
                                        jnp.pad(a[..., ff:], [(0, 0)] * (a.ndim - 1) + [(0, pad)])], axis=-1)
    return (halves(w_up).astype(BF16), halves(cw), halves(cb[None, :]),
            jnp.pad(w_down, [(0, pad), (0, 0)]).astype(BF16))


def kernel(x, c, ctx, c_ctx, w_mod, b_mod, norm1, w_in, rwkv_mu, rwkv_w0, rwkv_w2, rwkv_a0, rwkv_a2, rwkv_k_k, rwkv_k_a, rwkv_r_k, rwkv_g2, rwkv_ln_w, rwkv_ln_b, rwkv_v0, rwkv_v1, rwkv_v2, gqa_q_norm, gqa_k_norm, diff_lq1, diff_lk1, diff_lq2, diff_lk2, diff_subln, w_branch_r, w_branch_g, w_branch_d, w_out, norm2, ffn_up, ffn_conv_w, ffn_conv_b, ffn_down, final_norm):
    bsz, seq, d = x.shape
    ctx_len = ctx.shape[1]
    depth = w_in.shape[0]
    seg = ctx_len + seq
    m = bsz * seg
    seg_tiles = seg // SUB
    assert ctx_len == SUB and seq % SUB == 0 and bsz + 1 <= 8

    xs = jnp.concatenate([ctx, x], axis=1).reshape(m, d)

    pos = jnp.arange(m, dtype=jnp.int32) % seg
    pmask = ((pos != 0) & (pos != ctx_len)).astype(F32)[:, None]
    nmask = ((pos != ctx_len - 1) & (pos != seg - 1)).astype(F32)[:, None]

    tabs = _rope_tables(ctx_len, seq, G_HEAD) + _rope_tables(ctx_len, seq, DF_QK_HEAD)

    c_all = jnp.zeros((8, d), F32).at[:bsz].set(c).at[bsz].set(c_ctx)
    sub_idx = jnp.arange(m // SUB, dtype=jnp.int32)
    mod_row = jnp.where(sub_idx % seg_tiles == 0, bsz, sub_idx // seg_tiles)

    ffp = -(-ffn_down.shape[1] // FF_TILE) * FF_TILE
    tm_big = 1024 if m % 1024 == 0 else SUB
    tm_ffn = 512 if m % 512 == 0 else SUB

    v_first = None
    for i in range(depth):
        last = i == depth - 1
        mod = ada_mod(c_all, w_mod[i], b_mod[i])
        mod_t = mod[mod_row][:, None, :]

        p = norm_matmul(xs, norm1[i], mod_t, 1, 0, w_in[i].astype(BF16), tm=tm_big, tn=512)

        vres = None if i == 0 else (v_first, rwkv_v0[i - 1], rwkv_v1[i - 1], rwkv_v2[i - 1])
        r_t, v_t, kn_t, lw_t, kd_t, b_t, bonus, gate_r = rwkv_prep(
            p, pmask, nmask, rwkv_mu[i], rwkv_w0[i], rwkv_w2[i], rwkv_a0[i], rwkv_a2[i],
            rwkv_k_k[i], rwkv_k_a[i], rwkv_r_k[i], rwkv_g2[i], vres)
        if i == 0:
            v_first = v_t
        mh, ml, cm, rh, y0 = rwkv_phase1(r_t, v_t, kn_t, lw_t, kd_t, b_t)
        sf, sb = rwkv_phase2(mh, ml, cm, bsz=bsz, seg=seg, ctx_len=ctx_len)
        y_r = rwkv_phase3(rh, y0, sf, sb, bonus, gate_r, rwkv_ln_w[i], rwkv_ln_b[i])

        qg, kg, vg, dq, dk, dv = attn_prep(p, tabs, gqa_q_norm[i], gqa_k_norm[i], seg_tiles=seg_tiles)
        y_g = gqa_attention(qg, kg, vg, bsz=bsz, seg=seg, ctx_len=ctx_len)
        lam_init = 0.8 - 0.6 * math.exp(-0.3 * i)
        y_d = diff_attention(dq, dk, dv, diff_lq1[i], diff_lk1[i], diff_lq2[i], diff_lk2[i], diff_subln[i],
                             bsz=bsz, seg=seg, ctx_len=ctx_len, lam_init=lam_init)

        mrg = gated_merge(p, y_r, y_g, y_d, w_branch_r[i].astype(BF16), w_branch_g[i].astype(BF16),
                          w_branch_d[i].astype(BF16), tm=tm_big, tn=512)
        xs = out_proj_residual(mrg, w_out[i].astype(BF16), xs, mod_t, 2, tm=tm_big, tn=512)

        w_up, cw, cb, w_down = _pad_ffn(ffn_up[i], ffn_conv_w[i], ffn_conv_b[i], ffn_down[i], ffp)
        xs = conv_ffn(xs, norm2[i], mod_t, 4, 3, 5, pmask, nmask, w_up, cw, cb, w_down, final_norm,
                      tm=tm_ffn, final_norm=last)

    return xs.reshape(bsz, seg, d)[:, ctx_len:, :]
```

```python
import functools
import math

import jax
import jax.numpy as jnp
from jax import lax
from jax.experimental import pallas as pl
from jax.experimental.pallas import tpu as pltpu

F32 = jnp.float32
BF16 = jnp.bfloat16
LOG2E = 1.4426950408889634

GRID_W = 64
ROPE_BASE = 10000.0
NORM_EPS = 1e-6
R_HEADS, R_HEAD = 12, 64
R_WIDTH = R_HEADS * R_HEAD
R_DECAY_RANK, R_ICLR_RANK, R_GATE_RANK = 64, 64, 128
R_LN_EPS = 64e-5
G_HEADS, G_KV_HEADS, G_HEAD = 6, 2, 128
G_GROUP = G_HEADS // G_KV_HEADS
G_WIDTH = G_HEADS * G_HEAD
G_KV_WIDTH = G_KV_HEADS * G_HEAD
DF_HEADS, DF_QK_HEAD = 6, 64
DF_V_HEAD = 2 * DF_QK_HEAD
DF_WIDTH = DF_HEADS * DF_V_HEAD
DF_SUBLN_EPS = 1e-5
R_COLS = 3 * R_WIDTH + R_DECAY_RANK + R_ICLR_RANK + R_GATE_RANK
G_COLS = G_WIDTH + 2 * G_KV_WIDTH
DF_COLS = 3 * DF_WIDTH

LANES = 128
SUB = 256
CHUNK = 64
HALO = 16
FF_TILE = 512
Q_SPLIT = 128
VMEM_LIMIT = 56 * 1024 * 1024

PAIR = 2 * R_HEAD
N_PAIR = R_WIDTH // PAIR
assert PAIR == LANES and CHUNK == R_HEAD


def _cparams(sem):
    return pltpu.CompilerParams(dimension_semantics=sem, vmem_limit_bytes=VMEM_LIMIT)


def _mm(a, b, prec=None):
    return lax.dot_general(a, b, (((1,), (0,)), ((), ())), precision=prec, preferred_element_type=F32)


def _mm_nt(a, b, prec=None):
    return lax.dot_general(a, b, (((1,), (1,)), ((), ())), precision=prec, preferred_element_type=F32)


def _mm1(a, b):
    return _mm(a.astype(BF16), b.astype(BF16))


def _sigmoid(x):
    return 1.0 / (1.0 + jnp.exp(-x))


def _mod_kernel(c_ref, w_ref, b_ref, o_ref):
    c = c_ref[...]
    a = (c * _sigmoid(c)).astype(BF16)
    o_ref[...] = _mm(a, w_ref[...].astype(BF16)) + b_ref[...]


def ada_mod(c_all, w_stack, b, layer):
    rows, d = c_all.shape
    n = w_stack.shape[2]
    tn = 1024
    return pl.pallas_call(
        _mod_kernel,
        grid=(n // tn,),
        in_specs=[pl.BlockSpec((rows, d), lambda j: (0, 0)),
                  pl.BlockSpec((None, d, tn), lambda j: (layer, 0, j)),
                  pl.BlockSpec((1, tn), lambda j: (0, j))],
        out_specs=pl.BlockSpec((rows, tn), lambda j: (0, j)),
        out_shape=jax.ShapeDtypeStruct((rows, n), F32),
        compiler_params=_cparams(("parallel",)),
        name="ada_mod",
    )(c_all, w_stack, b.reshape(1, n))


def _modnorm(x, gain, sc, sh):
    ms = jnp.mean(x * x, axis=-1, keepdims=True)
    y = x * lax.rsqrt(ms + NORM_EPS) * gain
    return y * (1.0 + sc) + sh


def _normmm_kernel(x_ref, gain_ref, sc_ref, sh_ref, w_ref, o_ref, a_scr, *, nsub):
    @pl.when(pl.program_id(1) == 0)
    def _():
        for s in range(nsub):
            rows = pl.ds(s * SUB, SUB)
            a_scr[rows, :] = _modnorm(x_ref[rows, :], gain_ref[...], sc_ref[s], sh_ref[s]).astype(BF16)
    o_ref[...] = _mm(a_scr[...], w_ref[...])


def norm_matmul(x, gain, mod_t, sc_idx, sh_idx, w_bf, *, tm, tn):
    m, d = x.shape
    n = w_bf.shape[1]
    nsub = tm // SUB
    return pl.pallas_call(
        functools.partial(_normmm_kernel, nsub=nsub),
        grid=(m // tm, n // tn),
        in_specs=[pl.BlockSpec((tm, d), lambda i, j: (i, 0)),
                  pl.BlockSpec((1, d), lambda i, j: (0, 0)),
                  pl.BlockSpec((nsub, 1, d), lambda i, j: (i, 0, sc_idx)),
                  pl.BlockSpec((nsub, 1, d), lambda i, j: (i, 0, sh_idx)),
                  pl.BlockSpec((d, tn), lambda i, j: (0, j))],
        out_specs=pl.BlockSpec((tm, tn), lambda i, j: (i, j)),
        out_shape=jax.ShapeDtypeStruct((m, n), F32),
        scratch_shapes=[pltpu.VMEM((tm, d), BF16)],
        compiler_params=_cparams(("parallel", "arbitrary")),
        name="norm_matmul",
    )(x, gain.reshape(1, d), mod_t, mod_t, w_bf)


def _lane_partner(x, q):
    n = x.shape[-1]
    lane = lax.broadcasted_iota(jnp.int32, x.shape, x.ndim - 1)
    up = pltpu.roll(x, n - q, x.ndim - 1)
    dn = pltpu.roll(x, q, x.ndim - 1)
    return jnp.where((lane & q) == 0, up, dn)


def _group_allsum(x, group):
    r = lax.broadcasted_iota(jnp.int32, (LANES, LANES), 0)
    c = lax.broadcasted_iota(jnp.int32, (LANES, LANES), 1)
    ones_bd = jnp.where((r & -group) == (c & -group), 1.0, 0.0).astype(BF16)
    slabs = []
    for j in range(x.shape[-1] // LANES):
        xs = x[:, j * LANES:(j + 1) * LANES]
        hi = xs.astype(BF16)
        lo = (xs - hi.astype(F32)).astype(BF16)
        slabs.append(_mm(hi, ones_bd) + _mm(lo, ones_bd))
    return jnp.concatenate(slabs, axis=-1)


def _attn_prep_kernel(pg_ref, pdq_ref, pdk_ref, pdv_ref, cg_ref, sg_ref, cd_ref, sd_ref,
                      qn_ref, kn_ref, qg_ref, kg_ref, vg_ref, dq_ref, dk_ref, dv_ref):
    cg, sg = cg_ref[...], sg_ref[...]

    def head_norm_rope(x, gain, scale):
        ms = jnp.mean(x * x, axis=-1, keepdims=True)
        y = x * lax.rsqrt(ms + NORM_EPS) * gain
        y = y * cg + _lane_partner(y, G_HEAD // 4) * sg
        return y * scale

    g_scale = G_HEAD ** -0.5 * LOG2E
    for h in range(G_HEADS):
        cols = slice(h * G_HEAD, (h + 1) * G_HEAD)
        qg_ref[:, cols] = head_norm_rope(pg_ref[:, cols], qn_ref[...], g_scale).astype(BF16)
    for h in range(G_KV_HEADS):
        cols = slice(h * G_HEAD, (h + 1) * G_HEAD)
        src = slice(G_WIDTH + h * G_HEAD, G_WIDTH + (h + 1) * G_HEAD)
        kg_ref[:, cols] = head_norm_rope(pg_ref[:, src], kn_ref[...], 1.0).astype(BF16)
    vg_ref[...] = pg_ref[:, G_WIDTH + G_KV_WIDTH:].astype(BF16)

    cd, sd = cd_ref[...], sd_ref[...]
    d_scale = DF_QK_HEAD ** -0.5 * LOG2E
    for h in range(DF_HEADS):
        cols = slice(h * DF_V_HEAD, (h + 1) * DF_V_HEAD)
        xq = pdq_ref[:, cols]
        dq_ref[:, cols] = ((xq * cd + _lane_partner(xq, DF_QK_HEAD // 4) * sd) * d_scale).astype(BF16)
        xk = pdk_ref[:, cols]
        dk_ref[:, cols] = (xk * cd + _lane_partner(xk, DF_QK_HEAD // 4) * sd).astype(BF16)
    dv_ref[...] = pdv_ref[...].astype(BF16)


def attn_prep(p, tabs, q_gain, k_gain, *, seg_tiles):
    m = p.shape[0]
    cg, sg, cd, sd = tabs
    g_blk = R_COLS // G_COLS
    d_blk = (R_COLS + G_COLS) // DF_WIDTH
    assert g_blk * G_COLS == R_COLS and d_blk * DF_WIDTH == R_COLS + G_COLS
    row = lambda i: (i, 0)
    tab = lambda i: (i % seg_tiles, 0)
    one = lambda i: (0, 0)
    outs = [jax.ShapeDtypeStruct((m, w), BF16) for w in (G_WIDTH, G_KV_WIDTH, G_KV_WIDTH, DF_WIDTH, DF_WIDTH, DF_WIDTH)]
    return pl.pallas_call(
        _attn_prep_kernel,
        grid=(m // SUB,),
        in_specs=[pl.BlockSpec((SUB, G_COLS), lambda i: (i, g_blk)),
                  pl.BlockSpec((SUB, DF_WIDTH), lambda i: (i, d_blk)),
                  pl.BlockSpec((SUB, DF_WIDTH), lambda i: (i, d_blk + 1)),
                  pl.BlockSpec((SUB, DF_WIDTH), lambda i: (i, d_blk + 2)),
                  pl.BlockSpec((SUB, LANES), tab), pl.BlockSpec((SUB, LANES), tab),
                  pl.BlockSpec((SUB, LANES), tab), pl.BlockSpec((SUB, LANES), tab),
                  pl.BlockSpec((1, G_HEAD), one), pl.BlockSpec((1, G_HEAD), one)],
        out_specs=[pl.BlockSpec((SUB, o.shape[1]), row) for o in outs],
        out_shape=outs,
        compiler_params=_cparams(("parallel",)),
        name="attn_prep",
    )(p, p, p, p, cg, sg, cd, sd, q_gain.reshape(1, G_HEAD), k_gain.reshape(1, G_HEAD))


def _gqa_kernel(q_ref, k_ref, v_ref, o_ref, *, ctx_len):
    blocks = [slice(i * Q_SPLIT, (i + 1) * Q_SPLIT) for i in range(q_ref.shape[0] // Q_SPLIT)]

    def attend(klen):
        k, v = k_ref[0:klen, :], v_ref[0:klen, :]
        s = [_mm_nt(q_ref[rw, :], k) for rw in blocks]
        e = [jnp.exp2(x - jnp.max(x, axis=-1, keepdims=True)) for x in s]
        l = [jnp.sum(x, axis=-1, keepdims=True) for x in e]
        for rw, x, y in zip(blocks, e, l):
            o_ref[rw, :] = (_mm(x.astype(BF16), v) / y).astype(o_ref.dtype)

    is_ctx = pl.program_id(2) == 0
    pl.when(is_ctx)(lambda: attend(ctx_len))
    pl.when(jnp.logical_not(is_ctx))(lambda: attend(k_ref.shape[0]))


def gqa_attention(qg, kg, vg, *, bsz, seg, ctx_len):
    m = qg.shape[0]
    seg_tiles = seg // SUB
    k3 = kg.reshape(bsz, seg, G_KV_WIDTH)
    v3 = vg.reshape(bsz, seg, G_KV_WIDTH)
    qo = lambda b, h, t, g: (b * seg_tiles + t, h * G_GROUP + g)
    kv = lambda b, h, t, g: (b, 0, h)
    return pl.pallas_call(
        functools.partial(_gqa_kernel, ctx_len=ctx_len),
        grid=(bsz, G_KV_HEADS, seg_tiles, G_GROUP),
        in_specs=[pl.BlockSpec((SUB, G_HEAD), qo),
                  pl.BlockSpec((None, seg, G_HEAD), kv),
                  pl.BlockSpec((None, seg, G_HEAD), kv)],
        out_specs=pl.BlockSpec((SUB, G_HEAD), qo),
        out_shape=jax.ShapeDtypeStruct((m, G_WIDTH), BF16),
        compiler_params=_cparams(("parallel", "parallel", "parallel", "parallel")),
        name="gqa_attention",
    )(qg, k3, v3)


def _diff_kernel(q_ref, k_ref, v_ref, lq1_ref, lk1_ref, lq2_ref, lk2_ref, gain_ref, o_ref, *, ctx_len, lam_init):
    lane = lax.broadcasted_iota(jnp.int32, (Q_SPLIT, DF_V_HEAD), 1)
    zero = jnp.zeros((Q_SPLIT, DF_V_HEAD), q_ref.dtype)
    lam = (jnp.exp(jnp.sum(lq1_ref[...] * lk1_ref[...], axis=-1, keepdims=True))
           - jnp.exp(jnp.sum(lq2_ref[...] * lk2_ref[...], axis=-1, keepdims=True)) + lam_init)
    blocks = [slice(i * Q_SPLIT, (i + 1) * Q_SPLIT) for i in range(q_ref.shape[0] // Q_SPLIT)]

    def attend(klen):
        k, v = k_ref[0:klen, :], v_ref[0:klen, :]
        s0 = [_mm_nt(jnp.where(lane < DF_QK_HEAD, q_ref[rw, :], zero), k) for rw in blocks]
        s1 = [_mm_nt(jnp.where(lane >= DF_QK_HEAD, q_ref[rw, :], zero), k) for rw in blocks]
        e0 = [jnp.exp2(x - jnp.max(x, axis=-1, keepdims=True)) for x in s0]
        e1 = [jnp.exp2(x - jnp.max(x, axis=-1, keepdims=True)) for x in s1]
        c0 = [1.0 / jnp.sum(x, axis=-1, keepdims=True) for x in e0]
        c1 = [lam / jnp.sum(x, axis=-1, keepdims=True) for x in e1]
        for rw, x0, y0, x1, y1 in zip(blocks, e0, c0, e1, c1):
            o = _mm(x0.astype(BF16), v) * y0 - _mm(x1.astype(BF16), v) * y1
            ms = jnp.mean(o * o, axis=-1, keepdims=True)
            o = o * lax.rsqrt(ms + DF_SUBLN_EPS) * gain_ref[...] * (1.0 - lam_init)
            o_ref[rw, :] = o.astype(o_ref.dtype)

    is_ctx = pl.program_id(2) == 0
    pl.when(is_ctx)(lambda: attend(ctx_len))
    pl.when(jnp.logical_not(is_ctx))(lambda: attend(k_ref.shape[0]))


def diff_attention(dq, dk, dv, lq1, lk1, lq2, lk2, gain, *, bsz, seg, ctx_len, lam_init):
    m = dq.shape[0]
    seg_tiles = seg // SUB
    k3 = dk.reshape(bsz, seg, DF_WIDTH)
    v3 = dv.reshape(bsz, seg, DF_WIDTH)
    qo = lambda b, h, t: (b * seg_tiles + t, h)
    kv = lambda b, h, t: (b, 0, h)
    one = lambda b, h, t: (0, 0)
    vec = pl.BlockSpec((1, DF_QK_HEAD), one)
    return pl.pallas_call(
        functools.partial(_diff_kernel, ctx_len=ctx_len, lam_init=lam_init),
        grid=(bsz, DF_HEADS, seg_tiles),
        in_specs=[pl.BlockSpec((SUB, DF_V_HEAD), qo),
                  pl.BlockSpec((None, seg, DF_V_HEAD), kv),
                  pl.BlockSpec((None, seg, DF_V_HEAD), kv),
                  vec, vec, vec, vec, pl.BlockSpec((1, DF_V_HEAD), one)],
        out_specs=pl.BlockSpec((SUB, DF_V_HEAD), qo),
        out_shape=jax.ShapeDtypeStruct((m, DF_WIDTH), BF16),
        compiler_params=_cparams(("parallel", "parallel", "parallel")),
        name="diff_attention",
    )(dq, k3, v3, lq1.reshape(1, -1), lk1.reshape(1, -1), lq2.reshape(1, -1), lk2.reshape(1, -1),
      gain.reshape(1, -1))


def _rwkv_prep_kernel(*refs, has_vres):
    (pr_ref, prev_ref, next_ref, pm_ref, nm_ref, mu_ref, w0_ref, w2_ref, a0_ref, a2_ref,
     kk_ref, ka_ref, rk_ref, g2_ref) = refs[:14]
    refs = refs[14:]
    if has_vres:
        vf_ref, v0_ref, v1_ref, v2_ref = refs[:4]
        refs = refs[4:]
    r_out, v_out, kn_out, lw_out, kd_out, b_out, bonus_out, g_out = refs

    p = pr_ref[...]
    rows = lax.broadcasted_iota(jnp.int32, (SUB, 1), 0)
    prev = jnp.where(rows == 0, prev_ref[7:8, :], pltpu.roll(p, 1, 0)) * pm_ref[...]
    nxt = jnp.where(rows == SUB - 1, next_ref[0:1, :], pltpu.roll(p, SUB - 1, 0)) * nm_ref[...]
    xs = p + mu_ref[...] * (0.5 * (prev + nxt) - p)

    r = xs[:, 0:R_WIDTH]
    k = xs[:, R_WIDTH:2 * R_WIDTH]
    v = xs[:, 2 * R_WIDTH:3 * R_WIDTH]
    o = 3 * R_WIDTH
    w_lo = xs[:, o:o + R_DECAY_RANK]
    a_lo = xs[:, o + R_DECAY_RANK:o + R_DECAY_RANK + R_ICLR_RANK]
    g_lo = xs[:, o + R_DECAY_RANK + R_ICLR_RANK:]

    if has_vres:
        mix = _sigmoid(v0_ref[...] + _mm1(_mm1(v, v1_ref[...]), v2_ref[...]))
        v = v + (vf_ref[...] - v) * mix

    kk = k * kk_ref[...]
    nrm = jnp.sqrt(_group_allsum(kk * kk, R_HEAD))
    kk = kk / jnp.maximum(nrm, 1e-12)
    tw = jnp.tanh(w_lo)
    ksum = jnp.zeros_like(k)
    dec_scale = math.exp(-0.5)
    for d in range(2):
        z = w0_ref[d:d + 1, :] + _mm1(tw, w2_ref[d])
        lw_out[d] = -dec_scale * _sigmoid(z)
        a = _sigmoid(a0_ref[d:d + 1, :] + _mm1(a_lo, a2_ref[d]))
        k_d = k * (1.0 + (a - 1.0) * ka_ref[...])
        kd_out[d] = k_d
        b_out[d] = kk * a
        ksum = ksum + k_d
    r_out[...] = r
    v_out[...] = v
    kn_out[...] = kk
    bonus_out[...] = _group_allsum(r * ksum * rk_ref[...], R_HEAD) * v
    g_out[...] = _mm1(_sigmoid(g_lo), g2_ref[...])


def rwkv_prep(p, pmask, nmask, mu, w0, w2, a0, a2, k_k, k_a, r_k, g2, vres):
    m = p.shape[0]
    nt = m // SUB
    sub8 = SUB // 8
    last8 = m // 8 - 1
    one2 = lambda i: (0, 0)
    one3 = lambda i: (0, 0, 0)
    w = R_WIDTH
    in_specs = [pl.BlockSpec((SUB, R_COLS), lambda i: (i, 0)),
                pl.BlockSpec((8, R_COLS), lambda i: (jnp.maximum(i * sub8 - 1, 0), 0)),
                pl.BlockSpec((8, R_COLS), lambda i: (jnp.minimum((i + 1) * sub8, last8), 0)),
                pl.BlockSpec((SUB, 1), lambda i: (i, 0)),
                pl.BlockSpec((SUB, 1), lambda i: (i, 0)),
                pl.BlockSpec((1, R_COLS), one2),
                pl.BlockSpec((2, w), one2),
                pl.BlockSpec((2, R_DECAY_RANK, w), one3),
                pl.BlockSpec((2, w), one2),
                pl.BlockSpec((2, R_ICLR_RANK, w), one3),
                pl.BlockSpec((1, w), one2), pl.BlockSpec((1, w), one2), pl.BlockSpec((1, w), one2),
                pl.BlockSpec((R_GATE_RANK, w), one2)]
    args = [p, p, p, pmask, nmask, mu.reshape(1, -1), w0, w2, a0, a2, k_k.reshape(1, w), k_a.reshape(1, w),
            r_k.reshape(1, w), g2]
    if vres is not None:
        v_first, v0, v1, v2 = vres
        in_specs += [pl.BlockSpec((SUB, w), lambda i: (i, 0)), pl.BlockSpec((1, w), one2),
                     pl.BlockSpec(v1.shape, one2), pl.BlockSpec(v2.shape, one2)]
        args += [v_first, v0.reshape(1, w), v1, v2]
    ld = jax.ShapeDtypeStruct((m, w), F32)
    ld2 = jax.ShapeDtypeStruct((2, m, w), F32)
    ld_spec = pl.BlockSpec((SUB, w), lambda i: (i, 0))
    ld2_spec = pl.BlockSpec((2, SUB, w), lambda i: (0, i, 0))
    return pl.pallas_call(
        functools.partial(_rwkv_prep_kernel, has_vres=vres is not None),
        grid=(nt,),
        in_specs=in_specs,
        out_specs=[ld_spec, ld_spec, ld_spec, ld2_spec, ld2_spec, ld2_spec, ld_spec, ld_spec],
        out_shape=[ld, ld, ld, ld2, ld2, ld2, ld, ld],
        compiler_params=_cparams(("parallel",)),
        name="rwkv_prep",
    )(*args)


def _split2(a):
    hi = a.astype(BF16)
    lo = (a - hi.astype(F32)).astype(BF16)
    return hi, lo


def _split3(a):
    h1 = a.astype(BF16)
    r1 = a - h1.astype(F32)
    h2 = r1.astype(BF16)
    return h1, h2, (r1 - h2.astype(F32)).astype(BF16)


def _mm3(a, b, nt=False):
    f = _mm_nt if nt else _mm
    return f(a[0], b[0]) + (f(a[0], b[1]) + f(a[1], b[0]))


def _pair_consts(sign):
    t = lax.broadcasted_iota(jnp.int32, (CHUNK, PAIR), 0)
    lane = lax.broadcasted_iota(jnp.int32, (CHUNK, PAIR), 1)
    ahead = (t - (lane & (R_HEAD - 1))) * sign
    one, zero = jnp.ones((CHUNK, PAIR), F32), jnp.zeros((CHUNK, PAIR), F32)
    head0 = lane < R_HEAD
    t3 = lax.broadcasted_iota(jnp.int32, (CHUNK, 3 * CHUNK), 0)
    i3 = lax.broadcasted_iota(jnp.int32, (CHUNK, 3 * CHUNK), 1) & (CHUNK - 1)
    r2 = lax.broadcasted_iota(jnp.int32, (PAIR, PAIR), 0)
    l2 = lax.broadcasted_iota(jnp.int32, (PAIR, PAIR), 1)
    return dict(
        strict=ahead > 0, incl=ahead >= 0, eye=jnp.where(ahead == 0, one, zero), head0=head0,
        m0=jnp.where(head0, one, zero).astype(BF16), m1=jnp.where(head0, zero, one).astype(BF16),
        tri3=jnp.where((t3 - i3) * sign >= 0, 1.0, 0.0).astype(BF16),
        same_head=(r2 >= R_HEAD) == (l2 >= R_HEAD), eye2=(r2 == l2).astype(F32))


def _bd(x, c):
    return tuple(jnp.concatenate([u * c["m0"], u * c["m1"]], axis=0) for u in x)


def _chunks_affine(r, v, kk, lw, k, b, cs):
    each = lambda f, *xs: [f(*a) for a in zip(*xs)]
    c = cs[0]
    zero = jnp.zeros((CHUNK, PAIR), F32)
    bd = lambda x: _bd(x, c)
    mm3 = lambda x, y: _mm3(x, y)
    mm3_nt = lambda x, y: _mm3(x, y, nt=True)
    mm1 = lambda x, y: _mm(x[0], y[0])
    hi = lambda x: (x.astype(BF16),)
    top = lambda x: x[:CHUNK]
    bot = lambda x: x[CHUNK:]
    stack = lambda x, y: jnp.concatenate([x, y], axis=0)

    cum = each(lambda d, x: _mm(d["tri3"], jnp.concatenate(_split3(x), axis=0)), cs, lw)
    tot = each(lambda x: jnp.sum(x, axis=0, keepdims=True), lw)
    e_out = each(lambda x: jnp.exp(-x), cum)
    at = each(lambda kn, cu, x: -kn * jnp.exp(cu - x), kk, cum, lw)
    rt = each(lambda x, cu: x * jnp.exp(cu), r, cum)
    e_end = each(lambda t_, cu: jnp.exp(t_ - cu), tot, cum)
    v_bd = each(lambda x: bd(hi(x)), v)
    lhs = each(lambda x, y: _split2(stack(x, y)), at, rt)
    xb = each(mm3_nt, lhs, each(lambda x, e: bd(_split2(x * e)), b, e_out))
    xk = each(lambda x, y: _mm_nt(x[0], y[0]), lhs, each(lambda x, e: bd(hi(x * e)), k, e_out))
    a_ab = each(lambda d, x: jnp.where(d["strict"], top(x), zero), cs, xb)
    a_rb = each(lambda d, x: hi(jnp.where(d["incl"], bot(x), zero)), cs, xb)
    a_ak = each(lambda d, x: hi(jnp.where(d["strict"], top(x), zero)), cs, xk)
    a_rk = each(lambda d, x: hi(jnp.where(d["incl"], bot(x), zero)), cs, xk)
    t = each(lambda x: c["eye"] + x, a_ab)
    a_s = each(_split2, a_ab)
    pw = each(mm3, a_s, each(bd, a_s))
    for _ in range(int(math.log2(CHUNK)) - 2):
        x = each(mm3, each(lambda x, y: _split2(stack(x, y)), t, pw), each(lambda y: bd(_split2(y)), pw))
        t = each(lambda t_, x_: t_ + top(x_), t, x)
        pw = each(bot, x)
    t = each(lambda t_, p_: t_ + mm3(_split2(t_), bd(_split2(p_))), t, pw)
    t_s = each(_split2, t)
    ah = each(mm3, t_s, each(lambda x: bd(_split2(x)), at))
    x1 = each(mm1, a_ak, v_bd)
    u0 = each(mm3, t_s, each(lambda x: bd(_split2(x)), x1))
    rh = each(lambda r_, a_, h_: r_ + mm1(a_, bd(hi(h_))), rt, a_rb, ah)
    y0 = each(lambda a_, u_, k_, v_: mm1(a_, bd(hi(u_))) + mm1(k_, v_), a_rb, u0, a_rk, v_bd)
    bb = each(lambda x, e: _split2(x * e), b, e_end)
    kb = each(lambda x, e: _split2(x * e), k, e_end)
    m_full = each(lambda h_, b_: mm3(_split2(h_.T), b_), ah, bb)
    m_bd = each(lambda x, t_: jnp.where(c["same_head"], x, jnp.zeros_like(x)) + c["eye2"] * jnp.exp(t_), m_full, tot)
    c_full = each(lambda u_, b_, v_, k_: mm3(_split2(u_.T), b_) + mm3(_split2(v_.T), k_), u0, bb, v, kb)
    cm = each(lambda x: jnp.where(c["head0"], top(x), bot(x)), c_full)
    return m_bd, cm, rh, y0


def _rwkv_phase1_kernel(r_ref, v_ref, kn_ref, lw_ref, kd_ref, b_ref,
                        mh_out, ml_out, c_out, rh_out, y0_out):
    n = SUB // CHUNK
    rows = [slice(ci * CHUNK, (ci + 1) * CHUNK) for ci in range(n)]
    shared = lambda ref: [ref[rw, :] for rw in rows] * 2
    per_dir = lambda ref: [ref[d, rw, :] for d in range(2) for rw in rows]
    cs = [_pair_consts(1)] * n + [_pair_consts(-1)] * n
    m_bd, cm, rh, y0 = _chunks_affine(shared(r_ref), shared(v_ref), shared(kn_ref), per_dir(lw_ref),
                                      per_dir(kd_ref), per_dir(b_ref), cs)
    for d in range(2):
        for ci in range(n):
            i = d * n + ci
            mrows = slice(ci * PAIR, (ci + 1) * PAIR)
            mh_out[d, mrows, :], ml_out[d, mrows, :] = _split2(m_bd[i])
            c_out[d, rows[ci], :] = cm[i]
            rh_out[d, rows[ci], :] = rh[i].astype(BF16)
    for ci in range(n):
        y0_out[rows[ci], :] = y0[ci] + y0[n + ci]


def rwkv_phase1(r, v, kn, lw, kd, b):
    m, w = r.shape
    cpt = SUB // CHUNK
    s_spec = pl.BlockSpec((SUB, PAIR), lambda t, pr: (t, pr))
    d_spec = pl.BlockSpec((2, SUB, PAIR), lambda t, pr: (0, t, pr))
    m_spec = pl.BlockSpec((2, None, cpt * PAIR, PAIR), lambda t, pr: (0, pr, t, 0))
    m_shape = jax.ShapeDtypeStruct((2, N_PAIR, (m // CHUNK) * PAIR, PAIR), BF16)
    tok = lambda dt: jax.ShapeDtypeStruct((2, m, w), dt)
    return pl.pallas_call(
        _rwkv_phase1_kernel,
        grid=(m // SUB, N_PAIR),
        in_specs=[s_spec, s_spec, s_spec, d_spec, d_spec, d_spec],
        out_specs=[m_spec, m_spec, d_spec, d_spec, s_spec],
        out_shape=[m_shape, m_shape, tok(F32), tok(BF16), jax.ShapeDtypeStruct((m, w), F32)],
        compiler_params=_cparams(("parallel", "parallel")),
        name="rwkv_phase1",
    )(r, v, kn, lw, kd, b)


def _rwkv_phase2_kernel(mhf_ref, mlf_ref, cf_ref, mhb_ref, mlb_ref, cb_ref,
                        sf_out, sb_out, s_scr):
    @pl.when(pl.program_id(0) == 0)
    def _():
        s_scr[...] = jnp.zeros_like(s_scr)

    bsz = cf_ref.shape[0]
    dirs = ((mhf_ref, mlf_ref, cf_ref, sf_out), (mhb_ref, mlb_ref, cb_ref, sb_out))
    for d, (mh_ref, ml_ref, c_ref, s_out) in enumerate(dirs):
        for bi in range(bsz):
            for pr in range(N_PAIR):
                cols = slice(pr * PAIR, (pr + 1) * PAIR)
                s = _split2(s_scr[d, bi, :, cols])
                s_out[bi, :, cols] = s[0]
                s_scr[d, bi, :, cols] = _mm3(s, (mh_ref[pr, bi], ml_ref[pr, bi])) + c_ref[bi, :, cols]


def rwkv_phase2(mh, ml, cm, *, bsz, seg, ctx_len):
    w = cm.shape[-1]
    nchunk = seg // CHUNK
    nctx = ctx_len // CHUNK
    mh5 = mh.reshape(2, N_PAIR, bsz, nchunk * PAIR, PAIR)
    ml5 = ml.reshape(2, N_PAIR, bsz, nchunk * PAIR, PAIR)
    c4 = cm.reshape(2, bsz, seg, w)
    fwd = lambda s: s
    bwd = lambda s: jnp.where(s < nctx, nctx - 1 - s, nchunk - 1 + nctx - s)
    mblk = (None, N_PAIR, bsz, PAIR, PAIR)
    cblk = (None, bsz, CHUNK, w)
    m_spec = lambda d, order: pl.BlockSpec(mblk, lambda s: (d, 0, 0, order(s), 0))
    c_spec = lambda d, order: pl.BlockSpec(cblk, lambda s: (d, 0, order(s), 0))
    o_spec = lambda order: pl.BlockSpec((bsz, CHUNK, w), lambda s: (0, order(s), 0))
    out = jax.ShapeDtypeStruct((bsz, seg, w), BF16)
    outs = pl.pallas_call(
        _rwkv_phase2_kernel,
        grid=(nchunk,),
        in_specs=[m_spec(0, fwd), m_spec(0, fwd), c_spec(0, fwd), m_spec(1, bwd), m_spec(1, bwd), c_spec(1, bwd)],
        out_specs=[o_spec(fwd), o_spec(bwd)],
        out_shape=[out] * 2,
        scratch_shapes=[pltpu.VMEM((2, bsz, R_HEAD, w), F32)],
        compiler_params=_cparams(("arbitrary",)),
        name="rwkv_phase2",
    )(mh5, ml5, c4, mh5, ml5, c4)
    return [o.reshape(bsz * seg, w) for o in outs]


def _rwkv_phase3_kernel(rh_ref, y0_ref, sf_ref, sb_ref,
                        bonus_ref, g_ref, lnw_ref, lnb_ref, o_ref, y_scr):
    lane = lax.broadcasted_iota(jnp.int32, (CHUNK, PAIR), 1)
    one, zero = jnp.ones((CHUNK, PAIR), F32), jnp.zeros((CHUNK, PAIR), F32)
    c = dict(m0=jnp.where(lane < R_HEAD, one, zero).astype(BF16),
             m1=jnp.where(lane < R_HEAD, zero, one).astype(BF16))

    def body(ci, carry):
        rows = pl.ds(pl.multiple_of(ci * CHUNK, CHUNK), CHUNK)
        for pr in range(N_PAIR):
            cols = slice(pr * PAIR, (pr + 1) * PAIR)
            y = y0_ref[rows, cols]
            for d, s_ref in enumerate((sf_ref, sb_ref)):
                y = y + _mm_nt(rh_ref[d, rows, cols], _bd((s_ref[rows, cols],), c)[0])
            y_scr[rows, cols] = y
        return carry

    lax.fori_loop(0, SUB // CHUNK, body, 0)
    y = y_scr[...]
    inv_n = 1.0 / R_HEAD
    yc = y - _group_allsum(y, R_HEAD) * inv_n
    var = _group_allsum(yc * yc, R_HEAD) * inv_n
    yn = yc * lax.rsqrt(var + R_LN_EPS) * lnw_ref[...] + lnb_ref[...]
    o_ref[...] = ((yn + bonus_ref[...]) * g_ref[...]).astype(o_ref.dtype)


def rwkv_phase3(rh, y0, sf, sb, bonus, g, ln_w, ln_b):
    m, w = y0.shape
    d_spec = pl.BlockSpec((2, SUB, w), lambda t: (0, t, 0))
    ld_spec = pl.BlockSpec((SUB, w), lambda t: (t, 0))
    one = pl.BlockSpec((1, w), lambda t: (0, 0))
    return pl.pallas_call(
        _rwkv_phase3_kernel,
        grid=(m // SUB,),
        in_specs=[d_spec, ld_spec, ld_spec, ld_spec, ld_spec, ld_spec, one, one],
        out_specs=ld_spec,
        out_shape=jax.ShapeDtypeStruct((m, w), BF16),
        scratch_shapes=[pltpu.VMEM((SUB, w), F32)],
        compiler_params=_cparams(("parallel",)),
        name="rwkv_phase3",
    )(rh, y0, sf, sb, bonus, g, ln_w.reshape(1, w), ln_b.reshape(1, w))


def _merge_kernel(yr_ref, yg_ref, yd_ref, wr_ref, wg_ref, wd_ref, gr_ref, gg_ref, gd_ref, o_ref):
    acc = _sigmoid(gr_ref[...]) * _mm(yr_ref[...], wr_ref[...])
    acc = acc + _sigmoid(gg_ref[...]) * _mm(yg_ref[...], wg_ref[...])
    acc = acc + _sigmoid(gd_ref[...]) * _mm(yd_ref[...], wd_ref[...])
    o_ref[...] = acc.astype(o_ref.dtype)


def gated_merge(p, y_r, y_g, y_d, wr, wg, wd, *, tm, tn):
    m = p.shape[0]
    d = wr.shape[1]
    gate0 = (R_COLS + G_COLS + DF_COLS) // tn
    assert gate0 * tn == R_COLS + G_COLS + DF_COLS and d % tn == 0
    nb = d // tn
    y_spec = lambda w: pl.BlockSpec((tm, w), lambda i, j: (i, 0))
    w_spec = lambda w: pl.BlockSpec((w, tn), lambda i, j: (0, j))
    g_spec = lambda br: pl.BlockSpec((tm, tn), lambda i, j: (i, gate0 + br * nb + j))
    return pl.pallas_call(
        _merge_kernel,
        grid=(m // tm, nb),
        in_specs=[y_spec(R_WIDTH), y_spec(G_WIDTH), y_spec(DF_WIDTH),
                  w_spec(R_WIDTH), w_spec(G_WIDTH), w_spec(DF_WIDTH),
                  g_spec(0), g_spec(1), g_spec(2)],
        out_specs=pl.BlockSpec((tm, tn), lambda i, j: (i, j)),
        out_shape=jax.ShapeDtypeStruct((m, d), BF16),
        compiler_params=_cparams(("parallel", "parallel")),
        name="gated_merge",
    )(y_r, y_g, y_d, wr, wg, wd, p, p, p)


def _outproj_kernel(a_ref, w_ref, x_ref, g_ref, o_ref, *, nsub):
    y = _mm(a_ref[...], w_ref[...])
    for s in range(nsub):
        rows = pl.ds(s * SUB, SUB)
        o_ref[rows, :] = x_ref[rows, :] + g_ref[s] * y[s * SUB:(s + 1) * SUB, :]


def out_proj_residual(a, w_bf, x, mod_t, g_idx, *, tm, tn):
    m, d = x.shape
    nsub = tm // SUB
    gpb = d // tn
    return pl.pallas_call(
        functools.partial(_outproj_kernel, nsub=nsub),
        grid=(m // tm, d // tn),
        in_specs=[pl.BlockSpec((tm, a.shape[1]), lambda i, j: (i, 0)),
                  pl.BlockSpec((a.shape[1], tn), lambda i, j: (0, j)),
                  pl.BlockSpec((tm, tn), lambda i, j: (i, j)),
                  pl.BlockSpec((nsub, 1, tn), lambda i, j: (i, 0, g_idx * gpb + j))],
        out_specs=pl.BlockSpec((tm, tn), lambda i, j: (i, j)),
        out_shape=jax.ShapeDtypeStruct((m, d), F32),
        compiler_params=_cparams(("parallel", "parallel")),
        name="out_proj_residual",
    )(a, w_bf, x, mod_t)


def _ffn_kernel(x_ref, xp_ref, xn_ref, gain_ref, sc_ref, sh_ref, g_ref, pm_ref, nm_ref,
                wv_ref, wg_ref, cwv_ref, cwg_ref, cbv_ref, cbg_ref, wd_ref, fin_ref,
                o_ref, h_scr, acc_scr, uv_scr, ug_scr, *, nsub, final_norm):
    j = pl.program_id(1)
    tm = nsub * SUB

    @pl.when(j == 0)
    def _():
        gain = gain_ref[...]
        h_scr[0:HALO, :] = _modnorm(xp_ref[...], gain, sc_ref[0], sh_ref[0]).astype(BF16)
        for s in range(nsub):
            h_scr[pl.ds(HALO + s * SUB, SUB), :] = _modnorm(
                x_ref[pl.ds(s * SUB, SUB), :], gain, sc_ref[s], sh_ref[s]).astype(BF16)
        h_scr[HALO + tm:, :] = _modnorm(xn_ref[...], gain, sc_ref[nsub - 1], sh_ref[nsub - 1]).astype(BF16)
        acc_scr[...] = jnp.zeros_like(acc_scr)

    h = h_scr[...]
    uv_scr[...] = _mm(h, wv_ref[...])
    ug_scr[...] = _mm(h, wg_ref[...])
    pm, nm = pm_ref[...], nm_ref[...]

    def conv(u_scr, cw_ref, cb_ref):
        prev = u_scr[pl.ds(HALO - 1, tm), :] * pm
        cur = u_scr[pl.ds(HALO, tm), :]
        nxt = u_scr[pl.ds(HALO + 1, tm), :] * nm
        return prev * cw_ref[0:1, :] + cur * cw_ref[1:2, :] + nxt * cw_ref[2:3, :] + cb_ref[...]

    val = conv(uv_scr, cwv_ref, cbv_ref)
    gate = conv(ug_scr, cwg_ref, cbg_ref)
    act = (val * (gate * _sigmoid(gate))).astype(BF16)
    acc_scr[...] += _mm(act, wd_ref[...])

    @pl.when(j == pl.num_programs(1) - 1)
    def _():
        for s in range(nsub):
            rows = pl.ds(s * SUB, SUB)
            y = x_ref[rows, :] + g_ref[s] * acc_scr[rows, :]
            if final_norm:
                ms = jnp.mean(y * y, axis=-1, keepdims=True)
                y = y * lax.rsqrt(ms + NORM_EPS) * fin_ref[...]
            o_ref[rows, :] = y


def conv_ffn(x, gain, mod_t, sc_idx, sh_idx, g_idx, pmask, nmask, w_up, cw, cb, w_down, fin_gain,
             *, tm, final_norm):
    m, d = x.shape
    ffp = w_down.shape[0]
    nj = ffp // FF_TILE
    nsub = tm // SUB
    th = tm // HALO
    lasth = m // HALO - 1
    row = lambda i, j: (i, 0)
    one = lambda i, j: (0, 0)
    mod = lambda idx: pl.BlockSpec((nsub, 1, d), lambda i, j: (i, 0, idx))
    in_specs = [pl.BlockSpec((tm, d), row),
                pl.BlockSpec((HALO, d), lambda i, j: (jnp.maximum(i * th - 1, 0), 0)),
                pl.BlockSpec((HALO, d), lambda i, j: (jnp.minimum((i + 1) * th, lasth), 0)),
                pl.BlockSpec((1, d), one),
                mod(sc_idx), mod(sh_idx), mod(g_idx),
                pl.BlockSpec((tm, 1), row), pl.BlockSpec((tm, 1), row),
                pl.BlockSpec((d, FF_TILE), lambda i, j: (0, j)),
                pl.BlockSpec((d, FF_TILE), lambda i, j: (0, nj + j)),
                pl.BlockSpec((3, FF_TILE), lambda i, j: (0, j)),
                pl.BlockSpec((3, FF_TILE), lambda i, j: (0, nj + j)),
                pl.BlockSpec((1, FF_TILE), lambda i, j: (0, j)),
                pl.BlockSpec((1, FF_TILE), lambda i, j: (0, nj + j)),
                pl.BlockSpec((FF_TILE, d), lambda i, j: (j, 0)),
                pl.BlockSpec((1, d), one)]
    return pl.pallas_call(
        functools.partial(_ffn_kernel, nsub=nsub, final_norm=final_norm),
        grid=(m // tm, nj),
        in_specs=in_specs,
        out_specs=pl.BlockSpec((tm, d), row),
        out_shape=jax.ShapeDtypeStruct((m, d), F32),
        scratch_shapes=[pltpu.VMEM((tm + 2 * HALO, d), BF16), pltpu.VMEM((tm, d), F32),
                        pltpu.VMEM((tm + 2 * HALO, FF_TILE), F32), pltpu.VMEM((tm + 2 * HALO, FF_TILE), F32)],
        compiler_params=_cparams(("parallel", "arbitrary")),
        name="conv_ffn",
    )(x, x, x, gain.reshape(1, d), mod_t, mod_t, mod_t, pmask, nmask, w_up, w_up, cw, cw, cb, cb, w_down,
      fin_gain.reshape(1, d))


def _rope_tables(ctx_len, seq, head_dim):
    quarter = head_dim // 4
    t = jnp.arange(seq, dtype=jnp.int32)
    inv = ROPE_BASE ** (-jnp.arange(quarter, dtype=F32) / quarter)
    ang_r = (t // GRID_W).astype(F32)[:, None] * inv
    ang_c = (t % GRID_W).astype(F32)[:, None] * inv
    cos = jnp.concatenate([jnp.cos(ang_r)] * 2 + [jnp.cos(ang_c)] * 2, axis=-1)
    sin = jnp.concatenate([-jnp.sin(ang_r), jnp.sin(ang_r), -jnp.sin(ang_c), jnp.sin(ang_c)], axis=-1)
    reps = LANES // head_dim
    cos, sin = jnp.tile(cos, (1, reps)), jnp.tile(sin, (1, reps))
    cos = jnp.concatenate([jnp.ones((ctx_len, LANES), F32), cos], axis=0)
    sin = jnp.concatenate([jnp.zeros((ctx_len, LANES), F32), sin], axis=0)
    return cos, sin


def _pad_ffn(w_up, cw, cb, w_down, ffp):
    ff = w_down.shape[0]
    pad = ffp - ff
    halves = lambda a: jnp.concatenate([jnp.pad(a[..., :ff], [(0, 0)] * (a.ndim - 1) + [(0, pad)]),
                                        jnp.pad(a[..., ff:], [(0, 0)] * (a.ndim - 1) + [(0, pad)])], axis=-1)
    return (halves(w_up).astype(BF16), halves(cw), halves(cb[None, :]),
            jnp.pad(w_down, [(0, pad), (0, 0)]).astype(BF16))


def kernel(x, c, ctx, c_ctx, w_mod, b_mod, norm1, w_in, rwkv_mu, rwkv_w0, rwkv_w2, rwkv_a0, rwkv_a2, rwkv_k_k, rwkv_k_a, rwkv_r_k, rwkv_g2, rwkv_ln_w, rwkv_ln_b, rwkv_v0, rwkv_v1, rwkv_v2, gqa_q_norm, gqa_k_norm, diff_lq1, diff_lk1, diff_lq2, diff_lk2, diff_subln, w_branch_r, w_branch_g, w_branch_d, w_out, norm2, ffn_up, ffn_conv_w, ffn_conv_b, ffn_down, final_norm):
    bsz, seq, d = x.shape
    ctx_len = ctx.shape[1]
    depth = w_in.shape[0]
    seg = ctx_len + seq
    m = bsz * seg
    seg_tiles = seg // SUB
    assert ctx_len == SUB and seq % SUB == 0 and bsz + 1 <= 8

    xs = jnp.concatenate([ctx, x], axis=1).reshape(m, d)

    pos = jnp.arange(m, dtype=jnp.int32) % seg
    pmask = ((pos != 0) & (pos != ctx_len)).astype(F32)[:, None]
    nmask = ((pos != ctx_len - 1) & (pos != seg - 1)).astype(F32)[:, None]

    tabs = _rope_tables(ctx_len, seq, G_HEAD) + _rope_tables(ctx_len, seq, DF_QK_HEAD)

    c_all = jnp.zeros((8, d), F32).at[:bsz].set(c).at[bsz].set(c_ctx)
    sub_idx = jnp.arange(m // SUB, dtype=jnp.int32)
    mod_row = jnp.where(sub_idx % seg_tiles == 0, bsz, sub_idx // seg_tiles)

    ffp = -(-ffn_down.shape[1] // FF_TILE) * FF_TILE
    tm_big = 1024 if m % 1024 == 0 else SUB
    tm_ffn = 512 if m % 512 == 0 else SUB

    v_first = None
    for i in range(depth):
        last = i == depth - 1
        mod = ada_mod(c_all, w_mod, b_mod[i], i)
        mod_t = mod[mod_row][:, None, :]

        p = norm_matmul(xs, norm1[i], mod_t, 1, 0, w_in[i].astype(BF16), tm=tm_big, tn=512)

        vres = None if i == 0 else (v_first, rwkv_v0[i - 1], rwkv_v1[i - 1], rwkv_v2[i - 1])
        r_t, v_t, kn_t, lw_t, kd_t, b_t, bonus, gate_r = rwkv_prep(
            p, pmask, nmask, rwkv_mu[i], rwkv_w0[i], rwkv_w2[i], rwkv_a0[i], rwkv_a2[i],
            rwkv_k_k[i], rwkv_k_a[i], rwkv_r_k[i], rwkv_g2[i], vres)
        if i == 0:
            v_first = v_t
        mh, ml, cm, rh, y0 = rwkv_phase1(r_t, v_t, kn_t, lw_t, kd_t, b_t)
        sf, sb = rwkv_phase2(mh, ml, cm, bsz=bsz, seg=seg, ctx_len=ctx_len)
        y_r = rwkv_phase3(rh, y0, sf, sb, bonus, gate_r, rwkv_ln_w[i], rwkv_ln_b[i])

        qg, kg, vg, dq, dk, dv = attn_prep(p, tabs, gqa_q_norm[i], gqa_k_norm[i], seg_tiles=seg_tiles)
        y_g = gqa_attention(qg, kg, vg, bsz=bsz, seg=seg, ctx_len=ctx_len)
        lam_init = 0.8 - 0.6 * math.exp(-0.3 * i)
        y_d = diff_attention(dq, dk, dv, diff_lq1[i], diff_lk1[i], diff_lq2[i], diff_lk2[i], diff_subln[i],
                             bsz=bsz, seg=seg, ctx_len=ctx_len, lam_init=lam_init)

        mrg = gated_merge(p, y_r, y_g, y_d, w_branch_r[i].astype(BF16), w_branch_g[i].astype(BF16),
                          w_branch_d[i].astype(BF16), tm=tm_big, tn=512)
        xs = out_proj_residual(mrg, w_out[i].astype(BF16), xs, mod_t, 2, tm=tm_big, tn=512)

        w_up, cw, cb, w_down = _pad_ffn(ffn_up[i], ffn_conv_w[i], ffn_conv_b[i], ffn_down[i], ffp)
        xs = conv_ffn(xs, norm2[i], mod_t, 4, 3, 5, pmask, nmask, w_up, cw, cb, w_down, final_norm,
                      tm=tm_ffn, final_norm=last)

    return xs.reshape(bsz, seg, d)[:, ctx_len:, :]
```

```python
import functools
import math

import jax
import jax.numpy as jnp
from jax import lax
from jax.experimental import pallas as pl
from jax.experimental.pallas import tpu as pltpu

F32 = jnp.float32
BF16 = jnp.bfloat16
LOG2E = 1.4426950408889634

GRID_W = 64
ROPE_BASE = 10000.0
NORM_EPS = 1e-6
R_HEADS, R_HEAD = 12, 64
R_WIDTH = R_HEADS * R_HEAD
R_DECAY_RANK, R_ICLR_RANK, R_GATE_RANK = 64, 64, 128
R_LN_EPS = 64e-5
G_HEADS, G_KV_HEADS, G_HEAD = 6, 2, 128
G_GROUP = G_HEADS // G_KV_HEADS
G_WIDTH = G_HEADS * G_HEAD
G_KV_WIDTH = G_KV_HEADS * G_HEAD
DF_HEADS, DF_QK_HEAD = 6, 64
DF_V_HEAD = 2 * DF_QK_HEAD
DF_WIDTH = DF_HEADS * DF_V_HEAD
DF_SUBLN_EPS = 1e-5
R_COLS = 3 * R_WIDTH + R_DECAY_RANK + R_ICLR_RANK + R_GATE_RANK
G_COLS = G_WIDTH + 2 * G_KV_WIDTH
DF_COLS = 3 * DF_WIDTH

LANES = 128
SUB = 256
CHUNK = 64
HALO = 16
FF_TILE = 512
Q_SPLIT = 128
VMEM_LIMIT = 56 * 1024 * 1024

PAIR = 2 * R_HEAD
N_PAIR = R_WIDTH // PAIR
assert PAIR == LANES and CHUNK == R_HEAD


def _cparams(sem):
    return pltpu.CompilerParams(dimension_semantics=sem, vmem_limit_bytes=VMEM_LIMIT)


def _mm(a, b, prec=None):
    return lax.dot_general(a, b, (((1,), (0,)), ((), ())), precision=prec, preferred_element_type=F32)


def _mm_nt(a, b, prec=None):
    return lax.dot_general(a, b, (((1,), (1,)), ((), ())), precision=prec, preferred_element_type=F32)


def _mm1(a, b):
    return _mm(a.astype(BF16), b.astype(BF16))


def _sigmoid(x):
    return 1.0 / (1.0 + jnp.exp(-x))


def _mod_kernel(c_ref, w_ref, b_ref, o_ref):
    c = c_ref[...]
    a = (c * _sigmoid(c)).astype(BF16)
    o_ref[...] = _mm(a, w_ref[...].astype(BF16)) + b_ref[...]


def ada_mod(c_all, w_stack, b, layer):
    rows, d = c_all.shape
    n = w_stack.shape[2]
    tn = 1024
    return pl.pallas_call(
        _mod_kernel,
        grid=(n // tn,),
        in_specs=[pl.BlockSpec((rows, d), lambda j: (0, 0)),
                  pl.BlockSpec((None, d, tn), lambda j: (layer, 0, j)),
                  pl.BlockSpec((1, tn), lambda j: (0, j))],
        out_specs=pl.BlockSpec((rows, tn), lambda j: (0, j)),
        out_shape=jax.ShapeDtypeStruct((rows, n), F32),
        compiler_params=_cparams(("parallel",)),
        name="ada_mod",
    )(c_all, w_stack, b.reshape(1, n))


def _modnorm(x, gain, sc, sh):
    ms = jnp.mean(x * x, axis=-1, keepdims=True)
    y = x * lax.rsqrt(ms + NORM_EPS) * gain
    return y * (1.0 + sc) + sh


def _normmm_kernel(x_ref, gain_ref, sc_ref, sh_ref, w_ref, o_ref, a_scr, *, nsub):
    @pl.when(pl.program_id(1) == 0)
    def _():
        for s in range(nsub):
            rows = pl.ds(s * SUB, SUB)
            a_scr[rows, :] = _modnorm(x_ref[rows, :], gain_ref[...], sc_ref[s], sh_ref[s]).astype(BF16)
    o_ref[...] = _mm(a_scr[...], w_ref[...])


def norm_matmul(x, gain, mod_t, sc_idx, sh_idx, w_bf, *, tm, tn):
    m, d = x.shape
    n = w_bf.shape[1]
    nsub = tm // SUB
    return pl.pallas_call(
        functools.partial(_normmm_kernel, nsub=nsub),
        grid=(m // tm, n // tn),
        in_specs=[pl.BlockSpec((tm, d), lambda i, j: (i, 0)),
                  pl.BlockSpec((1, d), lambda i, j: (0, 0)),
                  pl.BlockSpec((nsub, 1, d), lambda i, j: (i, 0, sc_idx)),
                  pl.BlockSpec((nsub, 1, d), lambda i, j: (i, 0, sh_idx)),
                  pl.BlockSpec((d, tn), lambda i, j: (0, j))],
        out_specs=pl.BlockSpec((tm, tn), lambda i, j: (i, j)),
        out_shape=jax.ShapeDtypeStruct((m, n), F32),
        scratch_shapes=[pltpu.VMEM((tm, d), BF16)],
        compiler_params=_cparams(("parallel", "arbitrary")),
        name="norm_matmul",
    )(x, gain.reshape(1, d), mod_t, mod_t, w_bf)


def _lane_partner(x, q):
    n = x.shape[-1]
    lane = lax.broadcasted_iota(jnp.int32, x.shape, x.ndim - 1)
    up = pltpu.roll(x, n - q, x.ndim - 1)
    dn = pltpu.roll(x, q, x.ndim - 1)
    return jnp.where((lane & q) == 0, up, dn)


def _group_allsum(x, group):
    r = lax.broadcasted_iota(jnp.int32, (LANES, LANES), 0)
    c = lax.broadcasted_iota(jnp.int32, (LANES, LANES), 1)
    ones_bd = jnp.where((r & -group) == (c & -group), 1.0, 0.0).astype(BF16)
    slabs = []
    for j in range(x.shape[-1] // LANES):
        xs = x[:, j * LANES:(j + 1) * LANES]
        hi = xs.astype(BF16)
        lo = (xs - hi.astype(F32)).astype(BF16)
        slabs.append(_mm(hi, ones_bd) + _mm(lo, ones_bd))
    return jnp.concatenate(slabs, axis=-1)


def _attn_prep_kernel(pg_ref, pdq_ref, pdk_ref, pdv_ref, cg_ref, sg_ref, cd_ref, sd_ref,
                      qn_ref, kn_ref, qg_ref, kg_ref, vg_ref, dq_ref, dk_ref, dv_ref):
    cg, sg = cg_ref[...], sg_ref[...]

    def head_norm_rope(x, gain, scale):
        ms = jnp.mean(x * x, axis=-1, keepdims=True)
        y = x * lax.rsqrt(ms + NORM_EPS) * gain
        y = y * cg + _lane_partner(y, G_HEAD // 4) * sg
        return y * scale

    g_scale = G_HEAD ** -0.5 * LOG2E
    for h in range(G_HEADS):
        cols = slice(h * G_HEAD, (h + 1) * G_HEAD)
        qg_ref[:, cols] = head_norm_rope(pg_ref[:, cols], qn_ref[...], g_scale).astype(BF16)
    for h in range(G_KV_HEADS):
        cols = slice(h * G_HEAD, (h + 1) * G_HEAD)
        src = slice(G_WIDTH + h * G_HEAD, G_WIDTH + (h + 1) * G_HEAD)
        kg_ref[:, cols] = head_norm_rope(pg_ref[:, src], kn_ref[...], 1.0).astype(BF16)
    vg_ref[...] = pg_ref[:, G_WIDTH + G_KV_WIDTH:].astype(BF16)

    cd, sd = cd_ref[...], sd_ref[...]
    d_scale = DF_QK_HEAD ** -0.5 * LOG2E
    for h in range(DF_HEADS):
        cols = slice(h * DF_V_HEAD, (h + 1) * DF_V_HEAD)
        xq = pdq_ref[:, cols]
        dq_ref[:, cols] = ((xq * cd + _lane_partner(xq, DF_QK_HEAD // 4) * sd) * d_scale).astype(BF16)
        xk = pdk_ref[:, cols]
        dk_ref[:, cols] = (xk * cd + _lane_partner(xk, DF_QK_HEAD // 4) * sd).astype(BF16)
    dv_ref[...] = pdv_ref[...].astype(BF16)


def attn_prep(p, tabs, q_gain, k_gain, *, seg_tiles):
    m = p.shape[0]
    cg, sg, cd, sd = tabs
    g_blk = R_COLS // G_COLS
    d_blk = (R_COLS + G_COLS) // DF_WIDTH
    assert g_blk * G_COLS == R_COLS and d_blk * DF_WIDTH == R_COLS + G_COLS
    row = lambda i: (i, 0)
    tab = lambda i: (i % seg_tiles, 0)
    one = lambda i: (0, 0)
    outs = [jax.ShapeDtypeStruct((m, w), BF16) for w in (G_WIDTH, G_KV_WIDTH, G_KV_WIDTH, DF_WIDTH, DF_WIDTH, DF_WIDTH)]
    return pl.pallas_call(
        _attn_prep_kernel,
        grid=(m // SUB,),
        in_specs=[pl.BlockSpec((SUB, G_COLS), lambda i: (i, g_blk)),
                  pl.BlockSpec((SUB, DF_WIDTH), lambda i: (i, d_blk)),
                  pl.BlockSpec((SUB, DF_WIDTH), lambda i: (i, d_blk + 1)),
                  pl.BlockSpec((SUB, DF_WIDTH), lambda i: (i, d_blk + 2)),
                  pl.BlockSpec((SUB, LANES), tab), pl.BlockSpec((SUB, LANES), tab),
                  pl.BlockSpec((SUB, LANES), tab), pl.BlockSpec((SUB, LANES), tab),
                  pl.BlockSpec((1, G_HEAD), one), pl.BlockSpec((1, G_HEAD), one)],
        out_specs=[pl.BlockSpec((SUB, o.shape[1]), row) for o in outs],
        out_shape=outs,
        compiler_params=_cparams(("parallel",)),
        name="attn_prep",
    )(p, p, p, p, cg, sg, cd, sd, q_gain.reshape(1, G_HEAD), k_gain.reshape(1, G_HEAD))


def _gqa_kernel(q_ref, k_ref, v_ref, o_ref, *, ctx_len):
    blocks = [slice(i * Q_SPLIT, (i + 1) * Q_SPLIT) for i in range(q_ref.shape[0] // Q_SPLIT)]

    def attend(klen):
        k, v = k_ref[0:klen, :], v_ref[0:klen, :]
        s = [_mm_nt(q_ref[rw, :], k) for rw in blocks]
        e = [jnp.exp2(x - jnp.max(x, axis=-1, keepdims=True)) for x in s]
        l = [jnp.sum(x, axis=-1, keepdims=True) for x in e]
        for rw, x, y in zip(blocks, e, l):
            o_ref[rw, :] = (_mm(x.astype(BF16), v) / y).astype(o_ref.dtype)

    is_ctx = pl.program_id(2) == 0
    pl.when(is_ctx)(lambda: attend(ctx_len))
    pl.when(jnp.logical_not(is_ctx))(lambda: attend(k_ref.shape[0]))


def gqa_attention(qg, kg, vg, *, bsz, seg, ctx_len):
    m = qg.shape[0]
    seg_tiles = seg // SUB
    k3 = kg.reshape(bsz, seg, G_KV_WIDTH)
    v3 = vg.reshape(bsz, seg, G_KV_WIDTH)
    qo = lambda b, h, t, g: (b * seg_tiles + t, h * G_GROUP + g)
    kv = lambda b, h, t, g: (b, 0, h)
    return pl.pallas_call(
        functools.partial(_gqa_kernel, ctx_len=ctx_len),
        grid=(bsz, G_KV_HEADS, seg_tiles, G_GROUP),
        in_specs=[pl.BlockSpec((SUB, G_HEAD), qo),
                  pl.BlockSpec((None, seg, G_HEAD), kv),
                  pl.BlockSpec((None, seg, G_HEAD), kv)],
        out_specs=pl.BlockSpec((SUB, G_HEAD), qo),
        out_shape=jax.ShapeDtypeStruct((m, G_WIDTH), BF16),
        compiler_params=_cparams(("parallel", "parallel", "parallel", "parallel")),
        name="gqa_attention",
    )(qg, k3, v3)


def _diff_kernel(q_ref, k_ref, v_ref, lq1_ref, lk1_ref, lq2_ref, lk2_ref, gain_ref, o_ref, *, ctx_len, lam_init):
    lane = lax.broadcasted_iota(jnp.int32, (Q_SPLIT, DF_V_HEAD), 1)
    zero = jnp.zeros((Q_SPLIT, DF_V_HEAD), q_ref.dtype)
    lam = (jnp.exp(jnp.sum(lq1_ref[...] * lk1_ref[...], axis=-1, keepdims=True))
           - jnp.exp(jnp.sum(lq2_ref[...] * lk2_ref[...], axis=-1, keepdims=True)) + lam_init)
    blocks = [slice(i * Q_SPLIT, (i + 1) * Q_SPLIT) for i in range(q_ref.shape[0] // Q_SPLIT)]

    def attend(klen):
        k, v = k_ref[0:klen, :], v_ref[0:klen, :]
        s0 = [_mm_nt(jnp.where(lane < DF_QK_HEAD, q_ref[rw, :], zero), k) for rw in blocks]
        s1 = [_mm_nt(jnp.where(lane >= DF_QK_HEAD, q_ref[rw, :], zero), k) for rw in blocks]
        e0 = [jnp.exp2(x - jnp.max(x, axis=-1, keepdims=True)) for x in s0]
        e1 = [jnp.exp2(x - jnp.max(x, axis=-1, keepdims=True)) for x in s1]
        c0 = [1.0 / jnp.sum(x, axis=-1, keepdims=True) for x in e0]
        c1 = [lam / jnp.sum(x, axis=-1, keepdims=True) for x in e1]
        for rw, x0, y0, x1, y1 in zip(blocks, e0, c0, e1, c1):
            o = _mm(x0.astype(BF16), v) * y0 - _mm(x1.astype(BF16), v) * y1
            ms = jnp.mean(o * o, axis=-1, keepdims=True)
            o = o * lax.rsqrt(ms + DF_SUBLN_EPS) * gain_ref[...] * (1.0 - lam_init)
            o_ref[rw, :] = o.astype(o_ref.dtype)

    is_ctx = pl.program_id(2) == 0
    pl.when(is_ctx)(lambda: attend(ctx_len))
    pl.when(jnp.logical_not(is_ctx))(lambda: attend(k_ref.shape[0]))


def diff_attention(dq, dk, dv, lq1, lk1, lq2, lk2, gain, *, bsz, seg, ctx_len, lam_init):
    m = dq.shape[0]
    seg_tiles = seg // SUB
    k3 = dk.reshape(bsz, seg, DF_WIDTH)
    v3 = dv.reshape(bsz, seg, DF_WIDTH)
    qo = lambda b, h, t: (b * seg_tiles + t, h)
    kv = lambda b, h, t: (b, 0, h)
    one = lambda b, h, t: (0, 0)
    vec = pl.BlockSpec((1, DF_QK_HEAD), one)
    return pl.pallas_call(
        functools.partial(_diff_kernel, ctx_len=ctx_len, lam_init=lam_init),
        grid=(bsz, DF_HEADS, seg_tiles),
        in_specs=[pl.BlockSpec((SUB, DF_V_HEAD), qo),
                  pl.BlockSpec((None, seg, DF_V_HEAD), kv),
                  pl.BlockSpec((None, seg, DF_V_HEAD), kv),
                  vec, vec, vec, vec, pl.BlockSpec((1, DF_V_HEAD), one)],
        out_specs=pl.BlockSpec((SUB, DF_V_HEAD), qo),
        out_shape=jax.ShapeDtypeStruct((m, DF_WIDTH), BF16),
        compiler_params=_cparams(("parallel", "parallel", "parallel")),
        name="diff_attention",
    )(dq, k3, v3, lq1.reshape(1, -1), lk1.reshape(1, -1), lq2.reshape(1, -1), lk2.reshape(1, -1),
      gain.reshape(1, -1))


def _rwkv_prep_kernel(*refs, has_vres):
    (pr_ref, prev_ref, next_ref, pm_ref, nm_ref, mu_ref, w0_ref, w2_ref, a0_ref, a2_ref,
     kk_ref, ka_ref, rk_ref, g2_ref) = refs[:14]
    refs = refs[14:]
    if has_vres:
        vf_ref, v0_ref, v1_ref, v2_ref = refs[:4]
        refs = refs[4:]
    r_out, v_out, kn_out, lw_out, kd_out, b_out, bonus_out, g_out = refs

    p = pr_ref[...]
    rows = lax.broadcasted_iota(jnp.int32, (SUB, 1), 0)
    prev = jnp.where(rows == 0, prev_ref[7:8, :], pltpu.roll(p, 1, 0)) * pm_ref[...]
    nxt = jnp.where(rows == SUB - 1, next_ref[0:1, :], pltpu.roll(p, SUB - 1, 0)) * nm_ref[...]
    xs = p + mu_ref[...] * (0.5 * (prev + nxt) - p)

    r = xs[:, 0:R_WIDTH]
    k = xs[:, R_WIDTH:2 * R_WIDTH]
    v = xs[:, 2 * R_WIDTH:3 * R_WIDTH]
    o = 3 * R_WIDTH
    w_lo = xs[:, o:o + R_DECAY_RANK]
    a_lo = xs[:, o + R_DECAY_RANK:o + R_DECAY_RANK + R_ICLR_RANK]
    g_lo = xs[:, o + R_DECAY_RANK + R_ICLR_RANK:]

    if has_vres:
        mix = _sigmoid(v0_ref[...] + _mm1(_mm1(v, v1_ref[...]), v2_ref[...]))
        v = v + (vf_ref[...] - v) * mix

    kk = k * kk_ref[...]
    nrm = jnp.sqrt(_group_allsum(kk * kk, R_HEAD))
    kk = kk / jnp.maximum(nrm, 1e-12)
    tw = jnp.tanh(w_lo)
    ksum = jnp.zeros_like(k)
    dec_scale = math.exp(-0.5)
    for d in range(2):
        z = w0_ref[d:d + 1, :] + _mm1(tw, w2_ref[d])
        lw_out[d] = -dec_scale * _sigmoid(z)
        a = _sigmoid(a0_ref[d:d + 1, :] + _mm1(a_lo, a2_ref[d]))
        k_d = k * (1.0 + (a - 1.0) * ka_ref[...])
        kd_out[d] = k_d
        b_out[d] = kk * a
        ksum = ksum + k_d
    r_out[...] = r
    v_out[...] = v
    kn_out[...] = kk
    bonus_out[...] = _group_allsum(r * ksum * rk_ref[...], R_HEAD) * v
    g_out[...] = _mm1(_sigmoid(g_lo), g2_ref[...])


def rwkv_prep(p, pmask, nmask, mu, w0, w2, a0, a2, k_k, k_a, r_k, g2, vres):
    m = p.shape[0]
    nt = m // SUB
    sub8 = SUB // 8
    last8 = m // 8 - 1
    one2 = lambda i: (0, 0)
    one3 = lambda i: (0, 0, 0)
    w = R_WIDTH
    in_specs = [pl.BlockSpec((SUB, R_COLS), lambda i: (i, 0)),
                pl.BlockSpec((8, R_COLS), lambda i: (jnp.maximum(i * sub8 - 1, 0), 0)),
                pl.BlockSpec((8, R_COLS), lambda i: (jnp.minimum((i + 1) * sub8, last8), 0)),
                pl.BlockSpec((SUB, 1), lambda i: (i, 0)),
                pl.BlockSpec((SUB, 1), lambda i: (i, 0)),
                pl.BlockSpec((1, R_COLS), one2),
                pl.BlockSpec((2, w), one2),
                pl.BlockSpec((2, R_DECAY_RANK, w), one3),
                pl.BlockSpec((2, w), one2),
                pl.BlockSpec((2, R_ICLR_RANK, w), one3),
                pl.BlockSpec((1, w), one2), pl.BlockSpec((1, w), one2), pl.BlockSpec((1, w), one2),
                pl.BlockSpec((R_GATE_RANK, w), one2)]
    args = [p, p, p, pmask, nmask, mu.reshape(1, -1), w0, w2, a0, a2, k_k.reshape(1, w), k_a.reshape(1, w),
            r_k.reshape(1, w), g2]
    if vres is not None:
        v_first, v0, v1, v2 = vres
        in_specs += [pl.BlockSpec((SUB, w), lambda i: (i, 0)), pl.BlockSpec((1, w), one2),
                     pl.BlockSpec(v1.shape, one2), pl.BlockSpec(v2.shape, one2)]
        args += [v_first, v0.reshape(1, w), v1, v2]
    ld = jax.ShapeDtypeStruct((m, w), F32)
    ld2 = jax.ShapeDtypeStruct((2, m, w), F32)
    ld_spec = pl.BlockSpec((SUB, w), lambda i: (i, 0))
    ld2_spec = pl.BlockSpec((2, SUB, w), lambda i: (0, i, 0))
    return pl.pallas_call(
        functools.partial(_rwkv_prep_kernel, has_vres=vres is not None),
        grid=(nt,),
        in_specs=in_specs,
        out_specs=[ld_spec, ld_spec, ld_spec, ld2_spec, ld2_spec, ld2_spec, ld_spec, ld_spec],
        out_shape=[ld, ld, ld, ld2, ld2, ld2, ld, ld],
        compiler_params=_cparams(("parallel",)),
        name="rwkv_prep",
    )(*args)


def _split2(a):
    hi = a.astype(BF16)
    lo = (a - hi.astype(F32)).astype(BF16)
    return hi, lo


def _split3(a):
    h1 = a.astype(BF16)
    r1 = a - h1.astype(F32)
    h2 = r1.astype(BF16)
    return h1, h2, (r1 - h2.astype(F32)).astype(BF16)


def _mm3(a, b, nt=False):
    f = _mm_nt if nt else _mm
    return f(a[0], b[0]) + (f(a[0], b[1]) + f(a[1], b[0]))


def _pair_consts(sign):
    t = lax.broadcasted_iota(jnp.int32, (CHUNK, PAIR), 0)
    lane = lax.broadcasted_iota(jnp.int32, (CHUNK, PAIR), 1)
    ahead = (t - (lane & (R_HEAD - 1))) * sign
    one, zero = jnp.ones((CHUNK, PAIR), F32), jnp.zeros((CHUNK, PAIR), F32)
    head0 = lane < R_HEAD
    t3 = lax.broadcasted_iota(jnp.int32, (CHUNK, 3 * CHUNK), 0)
    i3 = lax.broadcasted_iota(jnp.int32, (CHUNK, 3 * CHUNK), 1) & (CHUNK - 1)
    r2 = lax.broadcasted_iota(jnp.int32, (PAIR, PAIR), 0)
    l2 = lax.broadcasted_iota(jnp.int32, (PAIR, PAIR), 1)
    return dict(
        strict=ahead > 0, incl=ahead >= 0, eye=jnp.where(ahead == 0, one, zero), head0=head0,
        m0=jnp.where(head0, one, zero).astype(BF16), m1=jnp.where(head0, zero, one).astype(BF16),
        tri3=jnp.where((t3 - i3) * sign >= 0, 1.0, 0.0).astype(BF16),
        same_head=(r2 >= R_HEAD) == (l2 >= R_HEAD), eye2=(r2 == l2).astype(F32))


def _bd(x, c):
    return tuple(jnp.concatenate([u * c["m0"], u * c["m1"]], axis=0) for u in x)


def _chunks_affine(r, v, kk, lw, k, b, cs):
    each = lambda f, *xs: [f(*a) for a in zip(*xs)]
    c = cs[0]
    zero = jnp.zeros((CHUNK, PAIR), F32)
    bd = lambda x: _bd(x, c)
    mm3 = lambda x, y: _mm3(x, y)
    mm3_nt = lambda x, y: _mm3(x, y, nt=True)
    mm1 = lambda x, y: _mm(x[0], y[0])
    hi = lambda x: (x.astype(BF16),)
    top = lambda x: x[:CHUNK]
    bot = lambda x: x[CHUNK:]
    stack = lambda x, y: jnp.concatenate([x, y], axis=0)

    cum = each(lambda d, x: _mm(d["tri3"], jnp.concatenate(_split3(x), axis=0)), cs, lw)
    tot = each(lambda x: jnp.sum(x, axis=0, keepdims=True), lw)
    e_out = each(lambda x: jnp.exp(-x), cum)
    at = each(lambda kn, cu, x: -kn * jnp.exp(cu - x), kk, cum, lw)
    rt = each(lambda x, cu: x * jnp.exp(cu), r, cum)
    e_end = each(lambda t_, cu: jnp.exp(t_ - cu), tot, cum)
    v_bd = each(lambda x: bd(hi(x)), v)
    lhs = each(lambda x, y: _split2(stack(x, y)), at, rt)
    xb = each(mm3_nt, lhs, each(lambda x, e: bd(_split2(x * e)), b, e_out))
    xk = each(lambda x, y: _mm_nt(x[0], y[0]), lhs, each(lambda x, e: bd(hi(x * e)), k, e_out))
    a_ab = each(lambda d, x: jnp.where(d["strict"], top(x), zero), cs, xb)
    a_rb = each(lambda d, x: hi(jnp.where(d["incl"], bot(x), zero)), cs, xb)
    a_ak = each(lambda d, x: hi(jnp.where(d["strict"], top(x), zero)), cs, xk)
    a_rk = each(lambda d, x: hi(jnp.where(d["incl"], bot(x), zero)), cs, xk)
    t = each(lambda x: c["eye"] + x, a_ab)
    a_s = each(_split2, a_ab)
    pw = each(mm3, a_s, each(bd, a_s))
    for _ in range(int(math.log2(CHUNK)) - 2):
        x = each(mm3, each(lambda x, y: _split2(stack(x, y)), t, pw), each(lambda y: bd(_split2(y)), pw))
        t = each(lambda t_, x_: t_ + top(x_), t, x)
        pw = each(bot, x)
    t = each(lambda t_, p_: t_ + mm3(_split2(t_), bd(_split2(p_))), t, pw)
    t_s = each(_split2, t)
    ah = each(mm3, t_s, each(lambda x: bd(_split2(x)), at))
    x1 = each(mm1, a_ak, v_bd)
    u0 = each(mm3, t_s, each(lambda x: bd(_split2(x)), x1))
    rh = each(lambda r_, a_, h_: r_ + mm1(a_, bd(hi(h_))), rt, a_rb, ah)
    y0 = each(lambda a_, u_, k_, v_: mm1(a_, bd(hi(u_))) + mm1(k_, v_), a_rb, u0, a_rk, v_bd)
    bb = each(lambda x, e: _split2(x * e), b, e_end)
    kb = each(lambda x, e: _split2(x * e), k, e_end)
    m_full = each(lambda h_, b_: mm3(_split2(h_.T), b_), ah, bb)
    m_bd = each(lambda x, t_: jnp.where(c["same_head"], x, jnp.zeros_like(x)) + c["eye2"] * jnp.exp(t_), m_full, tot)
    c_full = each(lambda u_, b_, v_, k_: mm3(_split2(u_.T), b_) + mm3(_split2(v_.T), k_), u0, bb, v, kb)
    cm = each(lambda x: jnp.where(c["head0"], top(x), bot(x)), c_full)
    return m_bd, cm, rh, y0


def _rwkv_phase1_kernel(r_ref, v_ref, kn_ref, lw_ref, kd_ref, b_ref,
                        mh_out, ml_out, c_out, rh_out, y0_out):
    n = SUB // CHUNK
    rows = [slice(ci * CHUNK, (ci + 1) * CHUNK) for ci in range(n)]
    shared = lambda ref: [ref[rw, :] for rw in rows] * 2
    per_dir = lambda ref: [ref[d, rw, :] for d in range(2) for rw in rows]
    cs = [_pair_consts(1)] * n + [_pair_consts(-1)] * n
    m_bd, cm, rh, y0 = _chunks_affine(shared(r_ref), shared(v_ref), shared(kn_ref), per_dir(lw_ref),
                                      per_dir(kd_ref), per_dir(b_ref), cs)
    for d in range(2):
        for ci in range(n):
            i = d * n + ci
            mrows = slice(ci * PAIR, (ci + 1) * PAIR)
            mh_out[d, mrows, :], ml_out[d, mrows, :] = _split2(m_bd[i])
            c_out[d, rows[ci], :] = cm[i]
            rh_out[d, rows[ci], :] = rh[i].astype(BF16)
    for ci in range(n):
        y0_out[rows[ci], :] = y0[ci] + y0[n + ci]


def rwkv_phase1(r, v, kn, lw, kd, b):
    m, w = r.shape
    cpt = SUB // CHUNK
    s_spec = pl.BlockSpec((SUB, PAIR), lambda t, pr: (t, pr))
    d_spec = pl.BlockSpec((2, SUB, PAIR), lambda t, pr: (0, t, pr))
    m_spec = pl.BlockSpec((2, None, cpt * PAIR, PAIR), lambda t, pr: (0, pr, t, 0))
    m_shape = jax.ShapeDtypeStruct((2, N_PAIR, (m // CHUNK) * PAIR, PAIR), BF16)
    tok = lambda dt: jax.ShapeDtypeStruct((2, m, w), dt)
    return pl.pallas_call(
        _rwkv_phase1_kernel,
        grid=(m // SUB, N_PAIR),
        in_specs=[s_spec, s_spec, s_spec, d_spec, d_spec, d_spec],
        out_specs=[m_spec, m_spec, d_spec, d_spec, s_spec],
        out_shape=[m_shape, m_shape, tok(F32), tok(BF16), jax.ShapeDtypeStruct((m, w), F32)],
        compiler_params=_cparams(("parallel", "parallel")),
        name="rwkv_phase1",
    )(r, v, kn, lw, kd, b)


def _rwkv_phase2_kernel(mhf_ref, mlf_ref, cf_ref, mhb_ref, mlb_ref, cb_ref,
                        sf_out, sb_out, s_scr):
    @pl.when(pl.program_id(0) == 0)
    def _():
        s_scr[...] = jnp.zeros_like(s_scr)

    bsz = cf_ref.shape[0]
    dirs = ((mhf_ref, mlf_ref, cf_ref, sf_out), (mhb_ref, mlb_ref, cb_ref, sb_out))
    for d, (mh_ref, ml_ref, c_ref, s_out) in enumerate(dirs):
        for bi in range(bsz):
            for pr in range(N_PAIR):
                cols = slice(pr * PAIR, (pr + 1) * PAIR)
                s = _split2(s_scr[d, bi, :, cols])
                s_out[bi, :, cols] = s[0]
                s_scr[d, bi, :, cols] = _mm3(s, (mh_ref[pr, bi], ml_ref[pr, bi])) + c_ref[bi, :, cols]


def rwkv_phase2(mh, ml, cm, *, bsz, seg, ctx_len):
    w = cm.shape[-1]
    nchunk = seg // CHUNK
    nctx = ctx_len // CHUNK
    mh5 = mh.reshape(2, N_PAIR, bsz, nchunk * PAIR, PAIR)
    ml5 = ml.reshape(2, N_PAIR, bsz, nchunk * PAIR, PAIR)
    c4 = cm.reshape(2, bsz, seg, w)
    fwd = lambda s: s
    bwd = lambda s: jnp.where(s < nctx, nctx - 1 - s, nchunk - 1 + nctx - s)
    mblk = (None, N_PAIR, bsz, PAIR, PAIR)
    cblk = (None, bsz, CHUNK, w)
    m_spec = lambda d, order: pl.BlockSpec(mblk, lambda s: (d, 0, 0, order(s), 0))
    c_spec = lambda d, order: pl.BlockSpec(cblk, lambda s: (d, 0, order(s), 0))
    o_spec = lambda order: pl.BlockSpec((bsz, CHUNK, w), lambda s: (0, order(s), 0))
    out = jax.ShapeDtypeStruct((bsz, seg, w), BF16)
    outs = pl.pallas_call(
        _rwkv_phase2_kernel,
        grid=(nchunk,),
        in_specs=[m_spec(0, fwd), m_spec(0, fwd), c_spec(0, fwd), m_spec(1, bwd), m_spec(1, bwd), c_spec(1, bwd)],
        out_specs=[o_spec(fwd), o_spec(bwd)],
        out_shape=[out] * 2,
        scratch_shapes=[pltpu.VMEM((2, bsz, R_HEAD, w), F32)],
        compiler_params=_cparams(("arbitrary",)),
        name="rwkv_phase2",
    )(mh5, ml5, c4, mh5, ml5, c4)
    return [o.reshape(bsz * seg, w) for o in outs]


def _rwkv_phase3_kernel(rh_ref, y0_ref, sf_ref, sb_ref,
                        bonus_ref, g_ref, lnw_ref, lnb_ref, o_ref, y_scr):
    lane = lax.broadcasted_iota(jnp.int32, (CHUNK, PAIR), 1)
    one, zero = jnp.ones((CHUNK, PAIR), F32), jnp.zeros((CHUNK, PAIR), F32)
    c = dict(m0=jnp.where(lane < R_HEAD, one, zero).astype(BF16),
             m1=jnp.where(lane < R_HEAD, zero, one).astype(BF16))

    def body(ci, carry):
        rows = pl.ds(pl.multiple_of(ci * CHUNK, CHUNK), CHUNK)
        for pr in range(N_PAIR):
            cols = slice(pr * PAIR, (pr + 1) * PAIR)
            y = y0_ref[rows, cols]
            for d, s_ref in enumerate((sf_ref, sb_ref)):
                y = y + _mm_nt(rh_ref[d, rows, cols], _bd((s_ref[rows, cols],), c)[0])
            y_scr[rows, cols] = y
        return carry

    lax.fori_loop(0, SUB // CHUNK, body, 0)
    y = y_scr[...]
    inv_n = 1.0 / R_HEAD
    yc = y - _group_allsum(y, R_HEAD) * inv_n
    var = _group_allsum(yc * yc, R_HEAD) * inv_n
    yn = yc * lax.rsqrt(var + R_LN_EPS) * lnw_ref[...] + lnb_ref[...]
    o_ref[...] = ((yn + bonus_ref[...]) * g_ref[...]).astype(o_ref.dtype)


def rwkv_phase3(rh, y0, sf, sb, bonus, g, ln_w, ln_b):
    m, w = y0.shape
    d_spec = pl.BlockSpec((2, SUB, w), lambda t: (0, t, 0))
    ld_spec = pl.BlockSpec((SUB, w), lambda t: (t, 0))
    one = pl.BlockSpec((1, w), lambda t: (0, 0))
    return pl.pallas_call(
        _rwkv_phase3_kernel,
        grid=(m // SUB,),
        in_specs=[d_spec, ld_spec, ld_spec, ld_spec, ld_spec, ld_spec, one, one],
        out_specs=ld_spec,
        out_shape=jax.ShapeDtypeStruct((m, w), BF16),
        scratch_shapes=[pltpu.VMEM((SUB, w), F32)],
        compiler_params=_cparams(("parallel",)),
        name="rwkv_phase3",
    )(rh, y0, sf, sb, bonus, g, ln_w.reshape(1, w), ln_b.reshape(1, w))


def _merge_kernel(yr_ref, yg_ref, yd_ref, wr_ref, wg_ref, wd_ref, gr_ref, gg_ref, gd_ref, o_ref):
    acc = _sigmoid(gr_ref[...]) * _mm(yr_ref[...], wr_ref[...])
    acc = acc + _sigmoid(gg_ref[...]) * _mm(yg_ref[...], wg_ref[...])
    acc = acc + _sigmoid(gd_ref[...]) * _mm(yd_ref[...], wd_ref[...])
    o_ref[...] = acc.astype(o_ref.dtype)


def gated_merge(p, y_r, y_g, y_d, wr, wg, wd, *, tm, tn):
    m = p.shape[0]
    d = wr.shape[1]
    gate0 = (R_COLS + G_COLS + DF_COLS) // tn
    assert gate0 * tn == R_COLS + G_COLS + DF_COLS and d % tn == 0
    nb = d // tn
    y_spec = lambda w: pl.BlockSpec((tm, w), lambda i, j: (i, 0))
    w_spec = lambda w: pl.BlockSpec((w, tn), lambda i, j: (0, j))
    g_spec = lambda br: pl.BlockSpec((tm, tn), lambda i, j: (i, gate0 + br * nb + j))
    return pl.pallas_call(
        _merge_kernel,
        grid=(m // tm, nb),
        in_specs=[y_spec(R_WIDTH), y_spec(G_WIDTH), y_spec(DF_WIDTH),
                  w_spec(R_WIDTH), w_spec(G_WIDTH), w_spec(DF_WIDTH),
                  g_spec(0), g_spec(1), g_spec(2)],
        out_specs=pl.BlockSpec((tm, tn), lambda i, j: (i, j)),
        out_shape=jax.ShapeDtypeStruct((m, d), BF16),
        compiler_params=_cparams(("parallel", "parallel")),
        name="gated_merge",
    )(y_r, y_g, y_d, wr, wg, wd, p, p, p)


def _outproj_kernel(a_ref, w_ref, x_ref, g_ref, o_ref, *, nsub):
    y = _mm(a_ref[...], w_ref[...])
    for s in range(nsub):
        rows = pl.ds(s * SUB, SUB)
        o_ref[rows, :] = x_ref[rows, :] + g_ref[s] * y[s * SUB:(s + 1) * SUB, :]


def out_proj_residual(a, w_bf, x, mod_t, g_idx, *, tm, tn):
    m, d = x.shape
    nsub = tm // SUB
    gpb = d // tn
    return pl.pallas_call(
        functools.partial(_outproj_kernel, nsub=nsub),
        grid=(m // tm, d // tn),
        in_specs=[pl.BlockSpec((tm, a.shape[1]), lambda i, j: (i, 0)),
                  pl.BlockSpec((a.shape[1], tn), lambda i, j: (0, j)),
                  pl.BlockSpec((tm, tn), lambda i, j: (i, j)),
                  pl.BlockSpec((nsub, 1, tn), lambda i, j: (i, 0, g_idx * gpb + j))],
        out_specs=pl.BlockSpec((tm, tn), lambda i, j: (i, j)),
        out_shape=jax.ShapeDtypeStruct((m, d), F32),
        compiler_params=_cparams(("parallel", "parallel")),
        name="out_proj_residual",
    )(a, w_bf, x, mod_t)


def _ffn_kernel(x_ref, xp_ref, xn_ref, gain_ref, sc_ref, sh_ref, g_ref, pm_ref, nm_ref,
                wv_ref, wg_ref, cwv_ref, cwg_ref, cbv_ref, cbg_ref, wd_ref, fin_ref,
                o_ref, h_scr, acc_scr, uv_scr, ug_scr, *, nsub, final_norm):
    j = pl.program_id(1)
    tm = nsub * SUB

    @pl.when(j == 0)
    def _():
        gain = gain_ref[...]
        h_scr[0:HALO, :] = _modnorm(xp_ref[...], gain, sc_ref[0], sh_ref[0]).astype(BF16)
        for s in range(nsub):
            h_scr[pl.ds(HALO + s * SUB, SUB), :] = _modnorm(
                x_ref[pl.ds(s * SUB, SUB), :], gain, sc_ref[s], sh_ref[s]).astype(BF16)
        h_scr[HALO + tm:, :] = _modnorm(xn_ref[...], gain, sc_ref[nsub - 1], sh_ref[nsub - 1]).astype(BF16)
        acc_scr[...] = jnp.zeros_like(acc_scr)

    h = h_scr[...]
    uv_scr[...] = _mm(h, wv_ref[...])
    ug_scr[...] = _mm(h, wg_ref[...])
    pm, nm = pm_ref[...], nm_ref[...]

    def conv(u_scr, cw_ref, cb_ref):
        prev = u_scr[pl.ds(HALO - 1, tm), :] * pm
        cur = u_scr[pl.ds(HALO, tm), :]
        nxt = u_scr[pl.ds(HALO + 1, tm), :] * nm
        return prev * cw_ref[0:1, :] + cur * cw_ref[1:2, :] + nxt * cw_ref[2:3, :] + cb_ref[...]

    val = conv(uv_scr, cwv_ref, cbv_ref)
    gate = conv(ug_scr, cwg_ref, cbg_ref)
    act = (val * (gate * _sigmoid(gate))).astype(BF16)
    acc_scr[...] += _mm(act, wd_ref[...])

    @pl.when(j == pl.num_programs(1) - 1)
    def _():
        for s in range(nsub):
            rows = pl.ds(s * SUB, SUB)
            y = x_ref[rows, :] + g_ref[s] * acc_scr[rows, :]
            if final_norm:
                ms = jnp.mean(y * y, axis=-1, keepdims=True)
                y = y * lax.rsqrt(ms + NORM_EPS) * fin_ref[...]
            o_ref[rows, :] = y


def conv_ffn(x, gain, mod_t, sc_idx, sh_idx, g_idx, pmask, nmask, w_up, cw, cb, w_down, fin_gain,
             *, tm, final_norm):
    m, d = x.shape
    ffp = w_down.shape[0]
    nj = ffp // FF_TILE
    nsub = tm // SUB
    th = tm // HALO
    lasth = m // HALO - 1
    row = lambda i, j: (i, 0)
    one = lambda i, j: (0, 0)
    mod = lambda idx: pl.BlockSpec((nsub, 1, d), lambda i, j: (i, 0, idx))
    in_specs = [pl.BlockSpec((tm, d), row),
                pl.BlockSpec((HALO, d), lambda i, j: (jnp.maximum(i * th - 1, 0), 0)),
                pl.BlockSpec((HALO, d), lambda i, j: (jnp.minimum((i + 1) * th, lasth), 0)),
                pl.BlockSpec((1, d), one),
                mod(sc_idx), mod(sh_idx), mod(g_idx),
                pl.BlockSpec((tm, 1), row), pl.BlockSpec((tm, 1), row),
                pl.BlockSpec((d, FF_TILE), lambda i, j: (0, j)),
                pl.BlockSpec((d, FF_TILE), lambda i, j: (0, nj + j)),
                pl.BlockSpec((3, FF_TILE), lambda i, j: (0, j)),
                pl.BlockSpec((3, FF_TILE), lambda i, j: (0, nj + j)),
                pl.BlockSpec((1, FF_TILE), lambda i, j: (0, j)),
                pl.BlockSpec((1, FF_TILE), lambda i, j: (0, nj + j)),
                pl.BlockSpec((FF_TILE, d), lambda i, j: (j, 0)),
                pl.BlockSpec((1, d), one)]
    return pl.pallas_call(
        functools.partial(_ffn_kernel, nsub=nsub, final_norm=final_norm),
        grid=(m // tm, nj),
        in_specs=in_specs,
        out_specs=pl.BlockSpec((tm, d), row),
        out_shape=jax.ShapeDtypeStruct((m, d), F32),
        scratch_shapes=[pltpu.VMEM((tm + 2 * HALO, d), BF16), pltpu.VMEM((tm, d), F32),
                        pltpu.VMEM((tm + 2 * HALO, FF_TILE), F32), pltpu.VMEM((tm + 2 * HALO, FF_TILE), F32)],
        compiler_params=_cparams(("parallel", "arbitrary")),
        name="conv_ffn",
    )(x, x, x, gain.reshape(1, d), mod_t, mod_t, mod_t, pmask, nmask, w_up, w_up, cw, cw, cb, cb, w_down,
      fin_gain.reshape(1, d))


def _rope_tables(ctx_len, seq, head_dim):
    quarter = head_dim // 4
    t = jnp.arange(seq, dtype=jnp.int32)
    inv = ROPE_BASE ** (-jnp.arange(quarter, dtype=F32) / quarter)
    ang_r = (t // GRID_W).astype(F32)[:, None] * inv
    ang_c = (t % GRID_W).astype(F32)[:, None] * inv
    cos = jnp.concatenate([jnp.cos(ang_r)] * 2 + [jnp.cos(ang_c)] * 2, axis=-1)
    sin = jnp.concatenate([-jnp.sin(ang_r), jnp.sin(ang_r), -jnp.sin(ang_c), jnp.sin(ang_c)], axis=-1)
    reps = LANES // head_dim
    cos, sin = jnp.tile(cos, (1, reps)), jnp.tile(sin, (1, reps))
    cos = jnp.concatenate([jnp.ones((ctx_len, LANES), F32), cos], axis=0)
    sin = jnp.concatenate([jnp.zeros((ctx_len, LANES), F32), sin], axis=0)
    return cos, sin


def _pad_ffn(w_up, cw, cb, w_down, ffp):
    ff = w_down.shape[0]
    pad = ffp - ff
    halves = lambda a: jnp.concatenate([jnp.pad(a[..., :ff], [(0, 0)] * (a.ndim - 1) + [(0, pad)]),
                                        jnp.pad(a[..., ff:], [(0, 0)] * (a.ndim - 1) + [(0, pad)])], axis=-1)
    return (halves(w_up).astype(BF16), halves(cw), halves(cb[None, :]),
            jnp.pad(w_down, [(0, pad), (0, 0)]).astype(BF16))


def kernel(x, c, ctx, c_ctx, w_mod, b_mod, norm1, w_in, rwkv_mu, rwkv_w0, rwkv_w2, rwkv_a0, rwkv_a2, rwkv_k_k, rwkv_k_a, rwkv_r_k, rwkv_g2, rwkv_ln_w, rwkv_ln_b, rwkv_v0, rwkv_v1, rwkv_v2, gqa_q_norm, gqa_k_norm, diff_lq1, diff_lk1, diff_lq2, diff_lk2, diff_subln, w_branch_r, w_branch_g, w_branch_d, w_out, norm2, ffn_up, ffn_conv_w, ffn_conv_b, ffn_down, final_norm):
    bsz, seq, d = x.shape
    ctx_len = ctx.shape[1]
    depth = w_in.shape[0]
    seg = ctx_len + seq
    m = bsz * seg
    seg_tiles = seg // SUB
    assert ctx_len == SUB and seq % SUB == 0 and bsz + 1 <= 8

    xs = jnp.concatenate([ctx, x], axis=1).reshape(m, d)

    pos = jnp.arange(m, dtype=jnp.int32) % seg
    pmask = ((pos != 0) & (pos != ctx_len)).astype(F32)[:, None]
    nmask = ((pos != ctx_len - 1) & (pos != seg - 1)).astype(F32)[:, None]

    tabs = _rope_tables(ctx_len, seq, G_HEAD) + _rope_tables(ctx_len, seq, DF_QK_HEAD)

    c_all = jnp.zeros((8, d), F32).at[:bsz].set(c).at[bsz].set(c_ctx)
    sub_idx = jnp.arange(m // SUB, dtype=jnp.int32)
    mod_row = jnp.where(sub_idx % seg_tiles == 0, bsz, sub_idx // seg_tiles)

    ffp = -(-ffn_down.shape[1] // FF_TILE) * FF_TILE
    tm_big = 1024 if m % 1024 == 0 else SUB
    tm_ffn = 512 if m % 512 == 0 else SUB

    v_first = None
    for i in range(depth):
        last = i == depth - 1
        mod = ada_mod(c_all, w_mod, b_mod[i], i)
        mod_t = mod[mod_row][:, None, :]

        p = norm_matmul(xs, norm1[i], mod_t, 1, 0, w_in[i].astype(BF16), tm=tm_big, tn=1024)

        vres = None if i == 0 else (v_first, rwkv_v0[i - 1], rwkv_v1[i - 1], rwkv_v2[i - 1])
        r_t, v_t, kn_t, lw_t, kd_t, b_t, bonus, gate_r = rwkv_prep(
            p, pmask, nmask, rwkv_mu[i], rwkv_w0[i], rwkv_w2[i], rwkv_a0[i], rwkv_a2[i],
            rwkv_k_k[i], rwkv_k_a[i], rwkv_r_k[i], rwkv_g2[i], vres)
        if i == 0:
            v_first = v_t
        mh, ml, cm, rh, y0 = rwkv_phase1(r_t, v_t, kn_t, lw_t, kd_t, b_t)
        sf, sb = rwkv_phase2(mh, ml, cm, bsz=bsz, seg=seg, ctx_len=ctx_len)
        y_r = rwkv_phase3(rh, y0, sf, sb, bonus, gate_r, rwkv_ln_w[i], rwkv_ln_b[i])

        qg, kg, vg, dq, dk, dv = attn_prep(p, tabs, gqa_q_norm[i], gqa_k_norm[i], seg_tiles=seg_tiles)
        y_g = gqa_attention(qg, kg, vg, bsz=bsz, seg=seg, ctx_len=ctx_len)
        lam_init = 0.8 - 0.6 * math.exp(-0.3 * i)
        y_d = diff_attention(dq, dk, dv, diff_lq1[i], diff_lk1[i], diff_lq2[i], diff_lk2[i], diff_subln[i],
                             bsz=bsz, seg=seg, ctx_len=ctx_len, lam_init=lam_init)

        mrg = gated_merge(p, y_r, y_g, y_d, w_branch_r[i].astype(BF16), w_branch_g[i].astype(BF16),
                          w_branch_d[i].astype(BF16), tm=tm_big, tn=512)
        xs = out_proj_residual(mrg, w_out[i].astype(BF16), xs, mod_t, 2, tm=tm_big, tn=512)

        w_up, cw, cb, w_down = _pad_ffn(ffn_up[i], ffn_conv_w[i], ffn_conv_b[i], ffn_down[i], ffp)
        xs = conv_ffn(xs, norm2[i], mod_t, 4, 3, 5, pmask, nmask, w_up, cw, cb, w_down, final_norm,
                      tm=tm_ffn, final_norm=last)

    return xs.reshape(bsz, seg, d)[:, ctx_len:, :]
```
